```python
import math
import jax, jax.numpy as jnp
from jax import lax
import numpy as np

D_MODEL = 2048
BATCH = 4
SEQ = 4096
DEPTH = 2

N_TOKEN_MIXERS = 2
EPS = 1e-6
MASK_VALUE = -1e30

CHUNK = 128
GMLP_WIDTH = 2 * D_MODEL
GMLP_GROUPS = 8
GMLP_GROUP_DIM = GMLP_WIDTH // GMLP_GROUPS

HEAD_DIM = 64
N_HEADS = D_MODEL // HEAD_DIM
N_KV_HEADS = max(1, N_HEADS // 8)
GQA_GROUP = N_HEADS // N_KV_HEADS
WINDOW = 128
ATTN_BLOCK = 128
ROT_DIM = HEAD_DIM // 4
ROPE_THETA = 500000.0

D_FF = ((8 * D_MODEL // 3 + 255) // 256) * 256
N_EXPERTS = 8
TOP_K = 2
D_FF_EXPERT = 7 * D_MODEL // 2

N_EVEN = (DEPTH + 1) // 2
N_ODD = DEPTH // 2

kernel_name = 'hybrid_gmlp_swa_sink_moe_adaln'


def _rms(x, g):
    xf = x.astype(jnp.float32)
    y = xf * lax.rsqrt(jnp.mean(xf * xf, axis=-1, keepdims=True) + EPS)
    return (y * g.astype(jnp.float32)).astype(x.dtype)


def _layernorm(x, g, b):
    xf = x.astype(jnp.float32)
    mu = jnp.mean(xf, axis=-1, keepdims=True)
    xc = xf - mu
    var = jnp.mean(xc * xc, axis=-1, keepdims=True)
    y = xc * lax.rsqrt(var + EPS) * g.astype(jnp.float32) + b.astype(jnp.float32)
    return y.astype(x.dtype)


def _rope_partial(x, cos, sin):
    x1 = x[..., : ROT_DIM // 2]
    x2 = x[..., ROT_DIM // 2: ROT_DIM]
    xp = x[..., ROT_DIM:]
    return jnp.concatenate([x1 * cos - x2 * sin, x2 * cos + x1 * sin, xp], axis=-1)


def _swiglu(h, w_gu, w_d):
    g, u = jnp.split(h @ w_gu, 2, axis=-1)
    return (jax.nn.silu(g) * u) @ w_d


def _gmlp_mixer(h, w_in, b_in, ln_g, ln_b, w_s, b_s, w_out):
    B, S, _ = h.shape
    z = jax.nn.gelu(h @ w_in + b_in, approximate=False)
    u, v = jnp.split(z, 2, axis=-1)
    v = _layernorm(v, ln_g, ln_b)
    v = v.reshape(B, S // CHUNK, CHUNK, GMLP_GROUPS, GMLP_GROUP_DIM)
    causal = jnp.tril(jnp.ones((CHUNK, CHUNK), dtype=bool))
    ws = jnp.where(causal[None], w_s, 0.0).astype(v.dtype)
    mixed = jnp.einsum('gts,bcsgd->bctgd', ws, v) + b_s.T[:, :, None].astype(v.dtype)
    return (u * mixed.reshape(B, S, GMLP_WIDTH)) @ w_out


def _swa_sink_attention(h, w_qkv, q_norm_g, k_norm_g, sinks, w_o, cos, sin):
    B, S, _ = h.shape
    nb = S // ATTN_BLOCK
    qkv = h @ w_qkv
    q, k, v = jnp.split(qkv, [N_HEADS * HEAD_DIM, (N_HEADS + N_KV_HEADS) * HEAD_DIM], axis=-1)
    q = _rms(q.reshape(B, S, N_HEADS, HEAD_DIM), q_norm_g)
    k = _rms(k.reshape(B, S, N_KV_HEADS, HEAD_DIM), k_norm_g)
    v = v.reshape(B, S, N_KV_HEADS, HEAD_DIM)
    q = _rope_partial(q, cos, sin)
    k = _rope_partial(k, cos, sin)

    qb = q.reshape(B, nb, ATTN_BLOCK, N_KV_HEADS, GQA_GROUP, HEAD_DIM)
    pad = jnp.zeros((B, ATTN_BLOCK, N_KV_HEADS, HEAD_DIM), k.dtype)
    kp = jnp.concatenate([pad, k], axis=1).reshape(B, nb + 1, ATTN_BLOCK, N_KV_HEADS, HEAD_DIM)
    vp = jnp.concatenate([pad, v], axis=1).reshape(B, nb + 1, ATTN_BLOCK, N_KV_HEADS, HEAD_DIM)
    kw = jnp.concatenate([kp[:, :-1], kp[:, 1:]], axis=2)
    vw = jnp.concatenate([vp[:, :-1], vp[:, 1:]], axis=2)

    scores = jnp.einsum('bnqkgd,bnrkd->bnkgqr', qb, kw,
                        preferred_element_type=jnp.float32) * (HEAD_DIM ** -0.5)
    qi = jnp.arange(ATTN_BLOCK)[:, None]
    ri = jnp.arange(2 * ATTN_BLOCK)[None, :]
    diff = qi + ATTN_BLOCK - ri
    band = (diff >= 0) & (diff < WINDOW)
    key_pos = jnp.arange(nb)[:, None] * ATTN_BLOCK + ri - ATTN_BLOCK
    valid = band[None] & (key_pos >= 0)[:, None, :]
    scores = jnp.where(valid[None, :, None, None], scores, MASK_VALUE)

    sink = sinks.astype(jnp.float32).reshape(N_KV_HEADS, GQA_GROUP)[None, None, :, :, None, None]
    m = jnp.maximum(jnp.max(scores, axis=-1, keepdims=True), sink)
    p = jnp.exp(scores - m)
    probs = p / (jnp.sum(p, axis=-1, keepdims=True) + jnp.exp(sink - m))
    out = jnp.einsum('bnkgqr,bnrkd->bnqkgd', probs.astype(vw.dtype), vw)
    return out.reshape(B, S, N_HEADS * HEAD_DIM) @ w_o


def _moe(h, router_w, router_b, w_gate_up, w_down):
    logits = h.astype(jnp.float32) @ router_w.astype(jnp.float32) + router_b.astype(jnp.float32)
    top_vals, top_idx = lax.top_k(logits, TOP_K)
    top_w = jax.nn.softmax(top_vals, axis=-1)
    gates = jnp.sum(jax.nn.one_hot(top_idx, N_EXPERTS, dtype=jnp.float32) * top_w[..., None], axis=-2)
    gates = gates.astype(h.dtype)
    y = jnp.zeros_like(h)
    for e in range(N_EXPERTS):
        y = y + gates[..., e:e + 1] * _swiglu(h, w_gate_up[e], w_down[e])
    return y


def setup_inputs(seed: int = 0) -> dict:
    key = jax.random.key(seed)
    ks = jax.random.split(key, 26)

    def nrm(k, shape, scale):
        return jax.random.normal(k, shape, jnp.float32) * scale

    D = D_MODEL
    offsets = jax.random.randint(ks[2], (BATCH, 1), 0, 4096, dtype=jnp.int32)
    positions = offsets + jnp.arange(SEQ, dtype=jnp.int32)[None, :]
    qkv_out = (N_HEADS + 2 * N_KV_HEADS) * HEAD_DIM
    return {
        'x': nrm(ks[0], (BATCH, SEQ, D), 1.0),
        'c': nrm(ks[1], (BATCH, D), 1.0),
        'positions': positions,
        'ada_w': nrm(ks[3], (DEPTH, D, 6 * D), D ** -0.5),
        'ada_b': nrm(ks[4], (DEPTH, 6 * D), 0.02),
        'norm_g': 1.0 + nrm(ks[5], (DEPTH, 2, D), 0.02),
        'gmlp_w_in': nrm(ks[6], (N_EVEN, D, 2 * GMLP_WIDTH), D ** -0.5),
        'gmlp_b_in': nrm(ks[7], (N_EVEN, 2 * GMLP_WIDTH), 0.02),
        'gmlp_ln_g': 1.0 + nrm(ks[8], (N_EVEN, GMLP_WIDTH), 0.02),
        'gmlp_ln_b': nrm(ks[9], (N_EVEN, GMLP_WIDTH), 0.02),
        'gmlp_w_s': nrm(ks[10], (N_EVEN, GMLP_GROUPS, CHUNK, CHUNK), CHUNK ** -0.5),
        'gmlp_b_s': 1.0 + nrm(ks[11], (N_EVEN, GMLP_GROUPS, CHUNK), 0.02),
        'gmlp_w_out': nrm(ks[12], (N_EVEN, GMLP_WIDTH, D), GMLP_WIDTH ** -0.5),
        'attn_w_qkv': nrm(ks[13], (N_ODD, D, qkv_out), D ** -0.5),
        'attn_q_norm_g': 1.0 + nrm(ks[14], (N_ODD, HEAD_DIM), 0.02),
        'attn_k_norm_g': 1.0 + nrm(ks[15], (N_ODD, HEAD_DIM), 0.02),
        'attn_sinks': nrm(ks[16], (N_ODD, N_HEADS), 1.0),
        'attn_w_o': nrm(ks[17], (N_ODD, N_HEADS * HEAD_DIM, D), (N_HEADS * HEAD_DIM) ** -0.5),
        'ffn_w_gate_up': nrm(ks[18], (N_EVEN, D, 2 * D_FF), D ** -0.5),
        'ffn_w_down': nrm(ks[19], (N_EVEN, D_FF, D), D_FF ** -0.5),
        'moe_router_w': nrm(ks[20], (N_ODD, D, N_EXPERTS), D ** -0.5),
        'moe_router_b': nrm(ks[21], (N_ODD, N_EXPERTS), 0.01),
        'moe_w_gate_up': nrm(ks[22], (N_ODD, N_EXPERTS, D, 2 * D_FF_EXPERT), D ** -0.5),
        'moe_w_down': nrm(ks[23], (N_ODD, N_EXPERTS, D_FF_EXPERT, D), D_FF_EXPERT ** -0.5),
    }


def reference(x, c, positions, ada_w, ada_b, norm_g,
              gmlp_w_in, gmlp_b_in, gmlp_ln_g, gmlp_ln_b, gmlp_w_s, gmlp_b_s, gmlp_w_out,
              attn_w_qkv, attn_q_norm_g, attn_k_norm_g, attn_sinks, attn_w_o,
              ffn_w_gate_up, ffn_w_down,
              moe_router_w, moe_router_b, moe_w_gate_up, moe_w_down):
    sc = jax.nn.silu(c)
    inv_freq = ROPE_THETA ** (-jnp.arange(0, ROT_DIM, 2, dtype=jnp.float32) / ROT_DIM)
    ang = positions.astype(jnp.float32)[..., None] * inv_freq
    cos = jnp.cos(ang)[:, :, None, :].astype(x.dtype)
    sin = jnp.sin(ang)[:, :, None, :].astype(x.dtype)

    for i in range(DEPTH):
        j = i // 2
        mod = (sc @ ada_w[i] + ada_b[i])[:, None, :]
        sh1, sc1, g1, sh2, sc2, g2 = jnp.split(mod, 6, axis=-1)

        h = _rms(x, norm_g[i, 0]) * (1.0 + sc1) + sh1
        if i % N_TOKEN_MIXERS == 0:
            y = _gmlp_mixer(h, gmlp_w_in[j], gmlp_b_in[j], gmlp_ln_g[j], gmlp_ln_b[j],
                            gmlp_w_s[j], gmlp_b_s[j], gmlp_w_out[j])
        else:
            y = _swa_sink_attention(h, attn_w_qkv[j], attn_q_norm_g[j], attn_k_norm_g[j],
                                    attn_sinks[j], attn_w_o[j], cos, sin)
        x = x + g1 * y

        h = _rms(x, norm_g[i, 1]) * (1.0 + sc2) + sh2
        if i % 2 == 0:
            y = _swiglu(h, ffn_w_gate_up[j], ffn_w_down[j])
        else:
            y = _moe(h, moe_router_w[j], moe_router_b[j], moe_w_gate_up[j], moe_w_down[j])
        x = x + g2 * y
    return x
```

```python
import functools
import math

import jax
import jax.numpy as jnp
from jax import lax
from jax.experimental import pallas as pl
from jax.experimental.pallas import tpu as pltpu

F32 = jnp.float32
BF16 = jnp.bfloat16

D = 2048
BATCH = 4
SEQ = 4096
T = BATCH * SEQ
EPS = 1e-6
MASK_VALUE = -1e30

CHUNK = 128
GW = 2 * D
GROUPS = 8
GDIM = GW // GROUPS

HD = 64
NH = D // HD
NKV = 4
GQA = NH // NKV
BLK = 128
NB = SEQ // BLK
ROT = 16
THETA = 500000.0
QKV = (NH + 2 * NKV) * HD

DFF = 5632
NE = 8
DFE = 7168

LANES = 128
VMEM_LIMIT = 56 * 1024 * 1024

TM = 1024
TN = 512
TMG = 512
TMR = 512
TME = 512
TNE1 = 512
TNE2 = 512
TMC = 256
CHD = 128
R = 2 * T + NE * TME
NT = R // TME


def _cp(sem):
    return pltpu.CompilerParams(dimension_semantics=sem, vmem_limit_bytes=VMEM_LIMIT)


def _ada_kernel(c_ref, w_ref, b_ref, o_ref):
    c = c_ref[...]
    sc = (c * jax.nn.sigmoid(c)).astype(BF16)
    o_ref[0] = jnp.dot(sc, w_ref[0].astype(BF16), preferred_element_type=F32) + b_ref[0]


def _ada_mod(c_pad, ada_w, ada_b):
    depth = ada_w.shape[0]
    n = ada_w.shape[2]
    tn = 1024
    return pl.pallas_call(
        _ada_kernel,
        grid=(depth, n // tn),
        in_specs=[
            pl.BlockSpec((8, D), lambda l, j: (0, 0)),
            pl.BlockSpec((1, D, tn), lambda l, j: (l, 0, j)),
            pl.BlockSpec((1, 1, tn), lambda l, j: (l, 0, j)),
        ],
        out_specs=pl.BlockSpec((1, 8, tn), lambda l, j: (l, 0, j)),
        out_shape=jax.ShapeDtypeStruct((depth, 8, n), F32),
        compiler_params=_cp(("arbitrary", "arbitrary")),
        name="ada_mod",
    )(c_pad, ada_w, ada_b.reshape(depth, 1, n))


def _norm_rows(x, g, a, b):
    ms = jnp.mean(x * x, axis=-1, keepdims=True)
    return (x * lax.rsqrt(ms + EPS) * g) * a + b


def _norm_prologue(x_ref, g_ref, sc_ref, sh_ref, h_ref, rows=256):
    g = g_ref[...]
    a = 1.0 + sc_ref[0]
    b = sh_ref[0]
    for r0 in range(0, x_ref.shape[0], rows):
        h_ref[r0:r0 + rows, :] = _norm_rows(x_ref[r0:r0 + rows, :], g, a, b).astype(BF16)


def _norm_specs(tm):
    return [
        pl.BlockSpec((tm, D), lambda i, j: (i, 0)),
        pl.BlockSpec((1, D), lambda i, j: (0, 0)),
        pl.BlockSpec((1, 1, D), lambda i, j: (i * tm // SEQ, 0, 0)),
        pl.BlockSpec((1, 1, D), lambda i, j: (i * tm // SEQ, 0, 0)),
    ]


def _gmlp_in_kernel(x_ref, g_ref, sc_ref, sh_ref, w_ref, b_ref, z_ref, st_ref,
                    h_scr, s1_scr, s2_scr, *, nj, jv0, nv):
    j = pl.program_id(1)

    @pl.when(j == 0)
    def _():
        _norm_prologue(x_ref, g_ref, sc_ref, sh_ref, h_scr)
        s1_scr[...] = jnp.zeros_like(s1_scr)
        s2_scr[...] = jnp.zeros_like(s2_scr)

    acc = jnp.dot(h_scr[...], w_ref[...].astype(BF16), preferred_element_type=F32) + b_ref[...]
    z = 0.5 * acc * (1.0 + lax.erf(acc * (1.0 / math.sqrt(2.0))))
    z_ref[...] = z.astype(BF16)

    @pl.when(j >= jv0)
    def _():
        s1_scr[...] += jnp.sum(z, axis=-1, keepdims=True)
        s2_scr[...] += jnp.sum(z * z, axis=-1, keepdims=True)

    @pl.when(j == nj - 1)
    def _():
        mean = s1_scr[...] * (1.0 / nv)
        var = s2_scr[...] * (1.0 / nv) - mean * mean
        rstd = lax.rsqrt(var + EPS)
        lane = lax.broadcasted_iota(jnp.int32, st_ref.shape, 1)
        st_ref[...] = jnp.where(lane == 0, mean, jnp.where(lane == 1, rstd, 0.0))


def _gmlp_in(x, g, sc, sh, w, b):
    n = w.shape[1]
    nj = n // TN
    kern = functools.partial(_gmlp_in_kernel, nj=nj, jv0=GW // TN, nv=float(GW))
    return pl.pallas_call(
        kern,
        grid=(T // TM, nj),
        in_specs=_norm_specs(TM) + [
            pl.BlockSpec((D, TN), lambda i, j: (0, j)),
            pl.BlockSpec((1, TN), lambda i, j: (0, j)),
        ],
        out_specs=[
            pl.BlockSpec((TM, TN), lambda i, j: (i, j)),
            pl.BlockSpec((TM, LANES), lambda i, j: (i, 0)),
        ],
        out_shape=[
            jax.ShapeDtypeStruct((T, n), BF16),
            jax.ShapeDtypeStruct((T, LANES), F32),
        ],
        scratch_shapes=[
            pltpu.VMEM((TM, D), BF16),
            pltpu.VMEM((TM, 1), F32),
            pltpu.VMEM((TM, 1), F32),
        ],
        compiler_params=_cp(("arbitrary", "arbitrary")),
        name="gmlp_in",
    )(x, g, sc, sh, w, b.reshape(1, n))


def _gmlp_gate_kernel(u_ref, v_ref, st_ref, lng_ref, lnb_ref, ws_ref, bs_ref, o_ref):
    row = lax.broadcasted_iota(jnp.int32, (CHUNK, CHUNK), 0)
    col = lax.broadcasted_iota(jnp.int32, (CHUNK, CHUNK), 1)
    causal = row >= col
    ws = [jnp.where(causal, ws_ref[g], 0.0).astype(BF16) for g in range(GROUPS)]
    for c in range(TMG // CHUNK):
        rs = slice(c * CHUNK, (c + 1) * CHUNK)
        mean = st_ref[rs, 0:1]
        rstd = st_ref[rs, 1:2]
        for g in range(GROUPS):
            cs = slice(g * GDIM, (g + 1) * GDIM)
            vn = (v_ref[rs, cs].astype(F32) - mean) * rstd * lng_ref[:, cs] + lnb_ref[:, cs]
            mixed = jnp.dot(ws[g], vn.astype(BF16), preferred_element_type=F32) + bs_ref[g]
            o_ref[rs, cs] = (u_ref[rs, cs].astype(F32) * mixed).astype(BF16)


def _gmlp_gate(z, st, ln_g, ln_b, w_s, b_s):
    return pl.pallas_call(
        _gmlp_gate_kernel,
        grid=(T // TMG,),
        in_specs=[
            pl.BlockSpec((TMG, GW), lambda i: (i, 0)),
            pl.BlockSpec((TMG, GW), lambda i: (i, 1)),
            pl.BlockSpec((TMG, LANES), lambda i: (i, 0)),
            pl.BlockSpec((1, GW), lambda i: (0, 0)),
            pl.BlockSpec((1, GW), lambda i: (0, 0)),
            pl.BlockSpec((GROUPS, CHUNK, CHUNK), lambda i: (0, 0, 0)),
            pl.BlockSpec((GROUPS, CHUNK, 1), lambda i: (0, 0, 0)),
        ],
        out_specs=pl.BlockSpec((TMG, GW), lambda i: (i, 0)),
        out_shape=jax.ShapeDtypeStruct((T, GW), BF16),
        compiler_params=_cp(("arbitrary",)),
        name="gmlp_gate",
    )(z, z, st, ln_g.reshape(1, GW), ln_b.reshape(1, GW), w_s, b_s.reshape(GROUPS, CHUNK, 1))


def _res_gemm_kernel(a_ref, w_ref, x_ref, gate_ref, o_ref):
    y = jnp.dot(a_ref[...], w_ref[...].astype(BF16), preferred_element_type=F32)
    o_ref[...] = x_ref[...] + gate_ref[0] * y


def _res_gemm(a, w, x, gate, *, tn, name):
    k = a.shape[1]
    return pl.pallas_call(
        _res_gemm_kernel,
        grid=(T // TM, D // tn),
        in_specs=[
            pl.BlockSpec((TM, k), lambda i, j: (i, 0)),
            pl.BlockSpec((k, tn), lambda i, j: (0, j)),
            pl.BlockSpec((TM, tn), lambda i, j: (i, j)),
            pl.BlockSpec((1, 1, tn), lambda i, j: (i * TM // SEQ, 0, j)),
        ],
        out_specs=pl.BlockSpec((TM, tn), lambda i, j: (i, j)),
        out_shape=jax.ShapeDtypeStruct((T, D), F32),
        compiler_params=_cp(("arbitrary", "arbitrary")),
        name=name,
    )(a, w, x, gate)


def _ffn_in_kernel(x_ref, g_ref, sc_ref, sh_ref, wg_ref, wu_ref, o_ref, h_scr):
    @pl.when(pl.program_id(1) == 0)
    def _():
        _norm_prologue(x_ref, g_ref, sc_ref, sh_ref, h_scr)

    h = h_scr[...]
    gate = jnp.dot(h, wg_ref[...].astype(BF16), preferred_element_type=F32)
    up = jnp.dot(h, wu_ref[...].astype(BF16), preferred_element_type=F32)
    o_ref[...] = (gate * jax.nn.sigmoid(gate) * up).astype(BF16)


def _ffn_in(x, g, sc, sh, w_gu):
    nj = DFF // TN
    return pl.pallas_call(
        _ffn_in_kernel,
        grid=(T // TM, nj),
        in_specs=_norm_specs(TM) + [
            pl.BlockSpec((D, TN), lambda i, j: (0, j)),
            pl.BlockSpec((D, TN), lambda i, j: (0, j + nj)),
        ],
        out_specs=pl.BlockSpec((TM, TN), lambda i, j: (i, j)),
        out_shape=jax.ShapeDtypeStruct((T, DFF), BF16),
        scratch_shapes=[pltpu.VMEM((TM, D), BF16)],
        compiler_params=_cp(("arbitrary", "arbitrary")),
        name="ffn_in",
    )(x, g, sc, sh, w_gu, w_gu)


def _qkv_kernel(x_ref, g_ref, sc_ref, sh_ref, pos_ref, invf_ref, w_ref, gain_ref, flag_ref,
                ind_ref, indt_ref, o_ref, h_scr, ct_scr, s1_scr, s2_scr):
    @pl.when(pl.program_id(1) == 0)
    def _():
        _norm_prologue(x_ref, g_ref, sc_ref, sh_ref, h_scr)
        lane = lax.broadcasted_iota(jnp.int32, ct_scr.shape, 1)
        m = lane & (HD - 1)
        ang = pos_ref[...] * invf_ref[...]
        c = jnp.cos(ang)
        s = jnp.sin(ang)
        ct_scr[...] = jnp.where(m < ROT, c, 1.0)
        s1_scr[...] = jnp.where(m < ROT // 2, -s, 0.0)
        s2_scr[...] = jnp.where((m >= ROT // 2) & (m < ROT), s, 0.0)

    y = jnp.dot(h_scr[...], w_ref[...].astype(BF16), preferred_element_type=F32)
    y2 = y * y
    hi = y2.astype(BF16)
    lo = (y2 - hi.astype(F32)).astype(BF16)
    ind = ind_ref[...]
    ss = (jnp.dot(hi, ind, preferred_element_type=F32)
          + jnp.dot(lo, ind, preferred_element_type=F32))
    r = lax.rsqrt(ss * (1.0 / HD) + EPS)
    rhi = r.astype(BF16)
    rlo = (r - rhi.astype(F32)).astype(BF16)
    indt = indt_ref[...]
    rb = (jnp.dot(rhi, indt, preferred_element_type=F32)
          + jnp.dot(rlo, indt, preferred_element_type=F32))
    yn = y * rb * gain_ref[...]
    ct = ct_scr[...]
    s1 = s1_scr[...]
    s2 = s2_scr[...]
    parts = []
    for k in range(TN // LANES):
        yb = yn[:, k * LANES:(k + 1) * LANES]
        parts.append(yb * ct + pltpu.roll(yb, LANES - ROT // 2, 1) * s1
                     + pltpu.roll(yb, ROT // 2, 1) * s2)
    yr = jnp.concatenate(parts, axis=1)
    o_ref[...] = jnp.where(flag_ref[...] > 0.5, yr, y).astype(BF16)


def _qkv(x, g, sc, sh, pos, invf, w, gain, flag, ind, indt):
    return pl.pallas_call(
        _qkv_kernel,
        grid=(T // TM, QKV // TN),
        in_specs=_norm_specs(TM) + [
            pl.BlockSpec((TM, 1), lambda i, j: (i, 0)),
            pl.BlockSpec((1, LANES), lambda i, j: (0, 0)),
            pl.BlockSpec((D, TN), lambda i, j: (0, j)),
            pl.BlockSpec((1, TN), lambda i, j: (0, j)),
            pl.BlockSpec((1, TN), lambda i, j: (0, j)),
            pl.BlockSpec((TN, LANES), lambda i, j: (0, 0)),
            pl.BlockSpec((LANES, TN), lambda i, j: (0, 0)),
        ],
        out_specs=pl.BlockSpec((TM, TN), lambda i, j: (i, j)),
        out_shape=jax.ShapeDtypeStruct((T, QKV), BF16),
        scratch_shapes=[
            pltpu.VMEM((TM, D), BF16),
            pltpu.VMEM((TM, LANES), F32),
            pltpu.VMEM((TM, LANES), F32),
            pltpu.VMEM((TM, LANES), F32),
        ],
        compiler_params=_cp(("arbitrary", "arbitrary")),
        name="qkv",
    )(x, g, sc, sh, pos, invf, w, gain, flag, ind, indt)


def _attn_kernel(q_ref, kp_ref, kc_ref, vp_ref, vc_ref, sink_ref, o_ref):
    n = pl.program_id(1)
    rows = GQA * BLK
    qi = lax.broadcasted_iota(jnp.int32, (rows, 2 * BLK), 0) & (BLK - 1)
    ri = lax.broadcasted_iota(jnp.int32, (rows, 2 * BLK), 1)
    valid = (ri > qi) & (ri <= qi + BLK) & ((ri >= BLK) | (n > 0))
    for kv in range(NKV):
        q = q_ref[0, 0, kv]
        kw = jnp.concatenate([kp_ref[0, 0, kv], kc_ref[0, 0, kv]], axis=0)
        vw = jnp.concatenate([vp_ref[0, 0, kv], vc_ref[0, 0, kv]], axis=0)
        s = lax.dot_general(q, kw, (((1,), (1,)), ((), ())), preferred_element_type=F32)
        s = jnp.where(valid, s, MASK_VALUE)
        sink = sink_ref[kv]
        m = jnp.maximum(jnp.max(s, axis=-1, keepdims=True), sink)
        p = jnp.exp(s - m)
        den = jnp.sum(p, axis=-1, keepdims=True) + jnp.exp(sink - m)
        probs = (p * (1.0 / den)).astype(BF16)
        o_ref[0, 0, kv] = jnp.dot(probs, vw, preferred_element_type=F32).astype(BF16)


def _attention(q5, k5, v5, sink5):
    rows = GQA * BLK
    cur = lambda b, n: (b, n, 0, 0, 0)
    prev = lambda b, n: (b, jnp.maximum(n - 1, 0), 0, 0, 0)
    return pl.pallas_call(
        _attn_kernel,
        grid=(BATCH, NB),
        in_specs=[
            pl.BlockSpec((1, 1, NKV, rows, HD), cur),
            pl.BlockSpec((1, 1, NKV, BLK, HD), prev),
            pl.BlockSpec((1, 1, NKV, BLK, HD), cur),
            pl.BlockSpec((1, 1, NKV, BLK, HD), prev),
            pl.BlockSpec((1, 1, NKV, BLK, HD), cur),
            pl.BlockSpec((NKV, rows, 1), lambda b, n: (0, 0, 0)),
        ],
        out_specs=pl.BlockSpec((1, 1, NKV, rows, HD), cur),
        out_shape=jax.ShapeDtypeStruct((BATCH, NB, NKV, rows, HD), BF16),
        compiler_params=_cp(("arbitrary", "arbitrary")),
        name="attention",
    )(q5, k5, k5, v5, v5, sink5)


def _pack_bf16_pairs(h):
    bits = lax.bitcast_convert_type(h.astype(BF16).astype(F32), jnp.uint32)
    n = h.shape[1] // 2
    return bits[:, :n] | (bits[:, n:] >> 16)


def _unpack_bf16_pairs(w):
    hi = lax.bitcast_convert_type(w & jnp.uint32(0xFFFF0000), F32)
    lo = lax.bitcast_convert_type(w << 16, F32)
    return jnp.concatenate([hi, lo], axis=1).astype(BF16)


def _router_kernel(x_ref, g_ref, sc_ref, sh_ref, rw_ref, rb_ref, hp_ref, route_ref, cnt_ref,
                   run_scr):
    i = pl.program_id(0)

    @pl.when(i == 0)
    def _():
        run_scr[...] = jnp.zeros_like(run_scr)

    h = _norm_rows(x_ref[...], g_ref[...], 1.0 + sc_ref[0], sh_ref[0])
    hp_ref[...] = _pack_bf16_pairs(h)

    logits = jnp.dot(h, rw_ref[...], preferred_element_type=F32,
                     precision=lax.Precision.HIGHEST) + rb_ref[...]
    lane = lax.broadcasted_iota(jnp.int32, logits.shape, 1).astype(F32)
    neg = -jnp.inf
    lg = jnp.where(lane < NE, logits, neg)
    v0 = jnp.max(lg, axis=-1, keepdims=True)
    i0 = jnp.min(jnp.where(lg == v0, lane, float(LANES)), axis=-1, keepdims=True)
    lg2 = jnp.where(lane == i0, neg, lg)
    v1 = jnp.max(lg2, axis=-1, keepdims=True)
    i1 = jnp.min(jnp.where(lg2 == v1, lane, float(LANES)), axis=-1, keepdims=True)
    t = jnp.exp(v1 - v0)
    w0 = 1.0 / (1.0 + t)
    w1 = t / (1.0 + t)

    sel0 = lane == i0
    sel1 = lane == i1
    onehot = jnp.where(sel0 | sel1, 1.0, 0.0)
    tm = onehot.shape[0]
    rr = lax.broadcasted_iota(jnp.int32, (tm, tm), 0)
    cc = lax.broadcasted_iota(jnp.int32, (tm, tm), 1)
    lower = jnp.where(rr > cc, 1.0, 0.0).astype(BF16)
    before = jnp.dot(lower, onehot.astype(BF16), preferred_element_type=F32) + run_scr[...]
    rank0 = jnp.sum(jnp.where(sel0, before, 0.0), axis=-1, keepdims=True)
    rank1 = jnp.sum(jnp.where(sel1, before, 0.0), axis=-1, keepdims=True)
    run_scr[...] += jnp.sum(onehot, axis=0, keepdims=True)
    cnt_ref[...] = run_scr[...]

    route_ref[...] = jnp.where(
        lane == 0, i0, jnp.where(
            lane == 1, i1, jnp.where(
                lane == 2, rank0, jnp.where(
                    lane == 3, rank1, jnp.where(
                        lane == 4, w0, jnp.where(lane == 5, w1, 0.0))))))


def _router(x, g, sc, sh, rw_pad, rb_pad):
    return pl.pallas_call(
        _router_kernel,
        grid=(T // TMR,),
        in_specs=[
            pl.BlockSpec((TMR, D), lambda i: (i, 0)),
            pl.BlockSpec((1, D), lambda i: (0, 0)),
            pl.BlockSpec((1, 1, D), lambda i: (i * TMR // SEQ, 0, 0)),
            pl.BlockSpec((1, 1, D), lambda i: (i * TMR // SEQ, 0, 0)),
            pl.BlockSpec((D, LANES), lambda i: (0, 0)),
            pl.BlockSpec((1, LANES), lambda i: (0, 0)),
        ],
        out_specs=[
            pl.BlockSpec((TMR, D // 2), lambda i: (i, 0)),
            pl.BlockSpec((TMR, LANES), lambda i: (i, 0)),
            pl.BlockSpec((1, LANES), lambda i: (0, 0)),
        ],
        out_shape=[
            jax.ShapeDtypeStruct((T, D // 2), jnp.uint32),
            jax.ShapeDtypeStruct((T, LANES), F32),
            jax.ShapeDtypeStruct((1, LANES), F32),
        ],
        scratch_shapes=[pltpu.VMEM((1, LANES), F32)],
        compiler_params=_cp(("arbitrary",)),
        name="router",
    )(x, g, sc, sh, rw_pad, rb_pad)


def _dispatch_kernel(p0_ref, p1_ref, h_hbm, xs_in_hbm, xs_hbm, sem):
    del xs_in_hbm

    def row_copy(tok, dst, slot):
        return pltpu.make_async_copy(h_hbm.at[pl.ds(tok, 1)], xs_hbm.at[pl.ds(dst, 1)],
                                     sem.at[slot])

    def drain(slot):
        def body(_, carry):
            row_copy(0, 0, slot).wait()
            return carry
        lax.fori_loop(0, 2 * CHD, body, 0)

    def chunk(c, carry):
        slot = c % 2

        def issue(t, carry2):
            tok = c * CHD + t
            row_copy(tok, p0_ref[tok], slot).start()
            row_copy(tok, p1_ref[tok], slot).start()
            return carry2
        lax.fori_loop(0, CHD, issue, 0)

        @pl.when(c > 0)
        def _():
            drain(1 - slot)
        return carry

    nchunk = T // CHD
    lax.fori_loop(0, nchunk, chunk, 0)
    drain((nchunk - 1) % 2)


def _dispatch(pos0, pos1, hp, xs_zero):
    return pl.pallas_call(
        _dispatch_kernel,
        grid_spec=pltpu.PrefetchScalarGridSpec(
            num_scalar_prefetch=2,
            grid=(1,),
            in_specs=[
                pl.BlockSpec(memory_space=pl.ANY),
                pl.BlockSpec(memory_space=pl.ANY),
            ],
            out_specs=pl.BlockSpec(memory_space=pl.ANY),
            scratch_shapes=[pltpu.SemaphoreType.DMA((2,))],
        ),
        out_shape=jax.ShapeDtypeStruct((R, D // 2), jnp.uint32),
        input_output_aliases={3: 0},
        compiler_params=_cp(("arbitrary",)),
        name="moe_dispatch",
    )(pos0, pos1, hp, xs_zero)


def _moe_up_kernel(te_ref, first_ref, nu_ref, xs_ref, wg_ref, wu_ref, o_ref, wg_scr, wu_scr):
    i = pl.program_id(1)

    @pl.when(first_ref[i] == 1)
    def _():
        wg_scr[...] = wg_ref[0].astype(BF16)
        wu_scr[...] = wu_ref[0].astype(BF16)

    @pl.when(i < nu_ref[0])
    def _():
        x = _unpack_bf16_pairs(xs_ref[...])
        gate = jnp.dot(x, wg_scr[...], preferred_element_type=F32)
        up = jnp.dot(x, wu_scr[...], preferred_element_type=F32)
        o_ref[...] = (gate * jax.nn.sigmoid(gate) * up).astype(BF16)

    @pl.when(i >= nu_ref[0])
    def _():
        o_ref[...] = jnp.zeros_like(o_ref)


def _moe_up(te, first, nu, xs, w_gu):
    nj = DFE // TNE1
    row = lambda j, i, te, first, nu: (jnp.minimum(i, nu[0] - 1), 0)
    return pl.pallas_call(
        _moe_up_kernel,
        grid_spec=pltpu.PrefetchScalarGridSpec(
            num_scalar_prefetch=3,
            grid=(nj, NT),
            in_specs=[
                pl.BlockSpec((TME, D // 2), row),
                pl.BlockSpec((1, D, TNE1), lambda j, i, te, first, nu: (te[i], 0, j)),
                pl.BlockSpec((1, D, TNE1), lambda j, i, te, first, nu: (te[i], 0, j + nj)),
            ],
            out_specs=pl.BlockSpec((TME, TNE1), lambda j, i, te, first, nu: (i, j)),
            scratch_shapes=[pltpu.VMEM((D, TNE1), BF16), pltpu.VMEM((D, TNE1), BF16)],
        ),
        out_shape=jax.ShapeDtypeStruct((R, DFE), BF16),
        compiler_params=_cp(("arbitrary", "arbitrary")),
        name="moe_up",
    )(te, first, nu, xs, w_gu, w_gu)


def _moe_down_kernel(te_ref, first_ref, nu_ref, a_ref, w_ref, o_ref, w_scr):
    i = pl.program_id(1)

    @pl.when(first_ref[i] == 1)
    def _():
        w_scr[...] = w_ref[0].astype(BF16)

    @pl.when(i < nu_ref[0])
    def _():
        o_ref[...] = jnp.dot(a_ref[...], w_scr[...], preferred_element_type=F32)

    @pl.when(i >= nu_ref[0])
    def _():
        o_ref[...] = jnp.zeros_like(o_ref)


def _moe_down(te, first, nu, a, w_d):
    row = lambda j, i, te, first, nu: (jnp.minimum(i, nu[0] - 1), 0)
    return pl.pallas_call(
        _moe_down_kernel,
        grid_spec=pltpu.PrefetchScalarGridSpec(
            num_scalar_prefetch=3,
            grid=(D // TNE2, NT),
            in_specs=[
                pl.BlockSpec((TME, DFE), row),
                pl.BlockSpec((1, DFE, TNE2), lambda j, i, te, first, nu: (te[i], 0, j)),
            ],
            out_specs=pl.BlockSpec((TME, TNE2), lambda j, i, te, first, nu: (i, j)),
            scratch_shapes=[pltpu.VMEM((DFE, TNE2), BF16)],
        ),
        out_shape=jax.ShapeDtypeStruct((R, D), F32),
        compiler_params=pltpu.CompilerParams(
            dimension_semantics=("arbitrary", "arbitrary"),
            vmem_limit_bytes=60 * 1024 * 1024),
        name="moe_down",
    )(te, first, nu, a, w_d)


def _combine_kernel(p0_ref, p1_ref, ys_hbm, x_ref, route_ref, gate_ref, o_ref, buf, sem):
    i = pl.program_id(0)
    n = pl.num_programs(0)

    def row_copy(src, k, t, slot):
        return pltpu.make_async_copy(ys_hbm.at[pl.ds(src, 1)], buf.at[slot, k, pl.ds(t, 1)],
                                     sem.at[slot])

    def issue(step, slot):
        def body(t, carry):
            tok = step * TMC + t
            row_copy(p0_ref[tok], 0, t, slot).start()
            row_copy(p1_ref[tok], 1, t, slot).start()
            return carry
        lax.fori_loop(0, TMC, body, 0)

    def drain(slot):
        def body(_, carry):
            row_copy(0, 0, 0, slot).wait()
            return carry
        lax.fori_loop(0, 2 * TMC, body, 0)

    @pl.when(i == 0)
    def _():
        issue(0, 0)

    @pl.when(i + 1 < n)
    def _():
        issue(i + 1, (i + 1) % 2)

    slot = i % 2
    drain(slot)
    w0 = route_ref[:, 4:5]
    w1 = route_ref[:, 5:6]
    y = w0 * buf[slot, 0] + w1 * buf[slot, 1]
    o_ref[...] = x_ref[...] + gate_ref[0] * y


def _combine(pos0, pos1, ys, x, route, gate):
    return pl.pallas_call(
        _combine_kernel,
        grid_spec=pltpu.PrefetchScalarGridSpec(
            num_scalar_prefetch=2,
            grid=(T // TMC,),
            in_specs=[
                pl.BlockSpec(memory_space=pl.ANY),
                pl.BlockSpec((TMC, D), lambda i, p0, p1: (i, 0)),
                pl.BlockSpec((TMC, LANES), lambda i, p0, p1: (i, 0)),
                pl.BlockSpec((1, 1, D), lambda i, p0, p1: (i * TMC // SEQ, 0, 0)),
            ],
            out_specs=pl.BlockSpec((TMC, D), lambda i, p0, p1: (i, 0)),
            scratch_shapes=[
                pltpu.VMEM((2, 2, TMC, D), F32),
                pltpu.SemaphoreType.DMA((2,)),
            ],
        ),
        out_shape=jax.ShapeDtypeStruct((T, D), F32),
        compiler_params=_cp(("arbitrary",)),
        name="moe_combine",
    )(pos0, pos1, ys, x, route, gate)


def _routing_plan(route, counts):
    e0 = route[:, 0].astype(jnp.int32)
    e1 = route[:, 1].astype(jnp.int32)
    r0 = route[:, 2].astype(jnp.int32)
    r1 = route[:, 3].astype(jnp.int32)
    cnt = counts[0, :NE].astype(jnp.int32)
    ntile = (cnt + TME - 1) // TME
    tile_end = jnp.cumsum(ntile)
    offs = (tile_end - ntile) * TME
    nu = tile_end[-1]
    tid = jnp.arange(NT, dtype=jnp.int32)
    te_raw = jnp.sum((tid[:, None] >= tile_end[None, :]).astype(jnp.int32), axis=1)
    te_last = jnp.sum((nu - 1 >= tile_end).astype(jnp.int32))
    te = jnp.where(tid < nu, te_raw, te_last).astype(jnp.int32)
    prev = jnp.concatenate([jnp.full((1,), -1, jnp.int32), te[:-1]])
    first = ((te != prev) & (tid < nu)).astype(jnp.int32)
    pos0 = offs[e0] + r0
    pos1 = offs[e1] + r1
    return pos0, pos1, te, first, nu.reshape(1).astype(jnp.int32)


def kernel(x, c, positions, ada_w, ada_b, norm_g, gmlp_w_in, gmlp_b_in, gmlp_ln_g, gmlp_ln_b,
           gmlp_w_s, gmlp_b_s, gmlp_w_out, attn_w_qkv, attn_q_norm_g, attn_k_norm_g, attn_sinks,
           attn_w_o, ffn_w_gate_up, ffn_w_down, moe_router_w, moe_router_b, moe_w_gate_up,
           moe_w_down):
    xf = x.reshape(T, D)
    c_pad = jnp.concatenate([c, jnp.zeros((8 - BATCH, D), F32)], axis=0)
    mod = _ada_mod(c_pad, ada_w, ada_b)[:, :BATCH, :]

    def mods(layer):
        return [mod[layer, :, k * D:(k + 1) * D].reshape(BATCH, 1, D) for k in range(6)]

    sh1, sc1, g1, sh2, sc2, g2 = mods(0)
    z, st = _gmlp_in(xf, norm_g[0, 0].reshape(1, D), sc1, sh1, gmlp_w_in[0], gmlp_b_in[0])
    gated = _gmlp_gate(z, st, gmlp_ln_g[0], gmlp_ln_b[0], gmlp_w_s[0], gmlp_b_s[0])
    xf = _res_gemm(gated, gmlp_w_out[0], xf, g1, tn=TN, name="gmlp_out")
    act = _ffn_in(xf, norm_g[0, 1].reshape(1, D), sc2, sh2, ffn_w_gate_up[0])
    xf = _res_gemm(act, ffn_w_down[0], xf, g2, tn=256, name="ffn_out")

    sh1, sc1, g1, sh2, sc2, g2 = mods(1)
    pos = positions.reshape(T, 1).astype(F32)
    inv_freq = THETA ** (-jnp.arange(0, ROT, 2, dtype=F32) / ROT)
    invf = jnp.tile(inv_freq, LANES // (ROT // 2)).reshape(1, LANES)
    scale = HD ** -0.5
    gain = jnp.concatenate([jnp.tile(attn_q_norm_g[0] * scale, NH),
                            jnp.tile(attn_k_norm_g[0], NKV),
                            jnp.ones((NKV * HD,), F32)]).reshape(1, QKV)
    flag = jnp.concatenate([jnp.ones(((NH + NKV) * HD,), F32),
                            jnp.zeros((NKV * HD,), F32)]).reshape(1, QKV)
    head_of_col = jnp.arange(TN, dtype=jnp.int32) // HD
    ind = (head_of_col[:, None] == jnp.arange(LANES, dtype=jnp.int32)[None, :]).astype(BF16)
    qkv = _qkv(xf, norm_g[1, 0].reshape(1, D), sc1, sh1, pos, invf, attn_w_qkv[0], gain, flag,
               ind, ind.T)
    q5 = (qkv[:, :NH * HD].reshape(BATCH, NB, BLK, NKV, GQA, HD)
          .transpose(0, 1, 3, 4, 2, 5).reshape(BATCH, NB, NKV, GQA * BLK, HD))
    k5 = (qkv[:, NH * HD:(NH + NKV) * HD].reshape(BATCH, NB, BLK, NKV, HD)
          .transpose(0, 1, 3, 2, 4))
    v5 = (qkv[:, (NH + NKV) * HD:].reshape(BATCH, NB, BLK, NKV, HD)
          .transpose(0, 1, 3, 2, 4))
    sink5 = jnp.repeat(attn_sinks[0].astype(F32), BLK).reshape(NKV, GQA * BLK, 1)
    o5 = _attention(q5, k5, v5, sink5)
    o = (o5.reshape(BATCH, NB, NKV, GQA, BLK, HD).transpose(0, 1, 4, 2, 3, 5)
         .reshape(T, NH * HD))
    xf = _res_gemm(o, attn_w_o[0], xf, g1, tn=TN, name="attn_out")

    rw_pad = jnp.concatenate([moe_router_w[0].astype(F32),
                              jnp.zeros((D, LANES - NE), F32)], axis=1)
    rb_pad = jnp.concatenate([moe_router_b[0].astype(F32),
                              jnp.zeros((LANES - NE,), F32)]).reshape(1, LANES)
    hp, route, counts = _router(xf, norm_g[1, 1].reshape(1, D), sc2, sh2, rw_pad, rb_pad)
    pos0, pos1, te, first, nu = _routing_plan(route, counts)
    xs = _dispatch(pos0, pos1, hp, jnp.zeros((R, D // 2), jnp.uint32))
    hmid = _moe_up(te, first, nu, xs, moe_w_gate_up[0])
    ys = _moe_down(te, first, nu, hmid, moe_w_down[0])
    xf = _combine(pos0, pos1, ys, xf, route, g2)
    return xf.reshape(BATCH, SEQ, D)
```

```python
import functools
import math

import jax
import jax.numpy as jnp
from jax import lax
from jax.experimental import pallas as pl
from jax.experimental.pallas import tpu as pltpu

F32 = jnp.float32
BF16 = jnp.bfloat16

D = 2048
BATCH = 4
SEQ = 4096
T = BATCH * SEQ
EPS = 1e-6
MASK_VALUE = -1e30

CHUNK = 128
GW = 2 * D
GROUPS = 8
GDIM = GW // GROUPS

HD = 64
NH = D // HD
NKV = 4
GQA = NH // NKV
BLK = 128
NB = SEQ // BLK
ROT = 16
THETA = 500000.0
QKV = (NH + 2 * NKV) * HD

DFF = 5632
NE = 8
DFE = 7168

LANES = 128
VMEM_LIMIT = 56 * 1024 * 1024

TM = 1024
TN = 512
TMG = 512
TMR = 512
TME = 512
TNE1 = 512
TNE2 = 512
TMC = 256
TMD = 256
R = 2 * T + NE * TME
NT = R // TME


def _cp(sem):
    return pltpu.CompilerParams(dimension_semantics=sem, vmem_limit_bytes=VMEM_LIMIT)


def _ada_kernel(c_ref, w_ref, b_ref, o_ref):
    c = c_ref[...]
    sc = (c * jax.nn.sigmoid(c)).astype(BF16)
    o_ref[0] = jnp.dot(sc, w_ref[0].astype(BF16), preferred_element_type=F32) + b_ref[0]


def _ada_mod(c_pad, ada_w, ada_b):
    depth = ada_w.shape[0]
    n = ada_w.shape[2]
    tn = 1024
    return pl.pallas_call(
        _ada_kernel,
        grid=(depth, n // tn),
        in_specs=[
            pl.BlockSpec((8, D), lambda l, j: (0, 0)),
            pl.BlockSpec((1, D, tn), lambda l, j: (l, 0, j)),
            pl.BlockSpec((1, 1, tn), lambda l, j: (l, 0, j)),
        ],
        out_specs=pl.BlockSpec((1, 8, tn), lambda l, j: (l, 0, j)),
        out_shape=jax.ShapeDtypeStruct((depth, 8, n), F32),
        compiler_params=_cp(("arbitrary", "arbitrary")),
        name="ada_mod",
    )(c_pad, ada_w, ada_b.reshape(depth, 1, n))


def _norm_rows(x, g, a, b):
    ms = jnp.mean(x * x, axis=-1, keepdims=True)
    return (x * lax.rsqrt(ms + EPS) * g) * a + b


def _norm_prologue(x_ref, g_ref, sc_ref, sh_ref, h_ref, rows=256):
    g = g_ref[...]
    a = 1.0 + sc_ref[0]
    b = sh_ref[0]
    for r0 in range(0, x_ref.shape[0], rows):
        h_ref[r0:r0 + rows, :] = _norm_rows(x_ref[r0:r0 + rows, :], g, a, b).astype(BF16)


def _norm_specs(tm):
    return [
        pl.BlockSpec((tm, D), lambda i, j: (i, 0)),
        pl.BlockSpec((1, D), lambda i, j: (0, 0)),
        pl.BlockSpec((1, 1, D), lambda i, j: (i * tm // SEQ, 0, 0)),
        pl.BlockSpec((1, 1, D), lambda i, j: (i * tm // SEQ, 0, 0)),
    ]


def _gmlp_in_kernel(x_ref, g_ref, sc_ref, sh_ref, w_ref, b_ref, z_ref, st_ref,
                    h_scr, s1_scr, s2_scr, *, nj, jv0, nv):
    j = pl.program_id(1)

    @pl.when(j == 0)
    def _():
        _norm_prologue(x_ref, g_ref, sc_ref, sh_ref, h_scr)
        s1_scr[...] = jnp.zeros_like(s1_scr)
        s2_scr[...] = jnp.zeros_like(s2_scr)

    acc = jnp.dot(h_scr[...], w_ref[...], preferred_element_type=F32) + b_ref[...]
    z = 0.5 * acc * (1.0 + lax.erf(acc * (1.0 / math.sqrt(2.0))))
    z_ref[...] = z.astype(BF16)

    @pl.when(j >= jv0)
    def _():
        s1_scr[...] += jnp.sum(z, axis=-1, keepdims=True)
        s2_scr[...] += jnp.sum(z * z, axis=-1, keepdims=True)

    @pl.when(j == nj - 1)
    def _():
        mean = s1_scr[...] * (1.0 / nv)
        var = s2_scr[...] * (1.0 / nv) - mean * mean
        rstd = lax.rsqrt(var + EPS)
        lane = lax.broadcasted_iota(jnp.int32, st_ref.shape, 1)
        st_ref[...] = jnp.where(lane == 0, mean, jnp.where(lane == 1, rstd, 0.0))


def _gmlp_in(x, g, sc, sh, w, b):
    n = w.shape[1]
    nj = n // TN
    kern = functools.partial(_gmlp_in_kernel, nj=nj, jv0=GW // TN, nv=float(GW))
    return pl.pallas_call(
        kern,
        grid=(T // TM, nj),
        in_specs=_norm_specs(TM) + [
            pl.BlockSpec((D, TN), lambda i, j: (0, j)),
            pl.BlockSpec((1, TN), lambda i, j: (0, j)),
        ],
        out_specs=[
            pl.BlockSpec((TM, TN), lambda i, j: (i, j)),
            pl.BlockSpec((TM, LANES), lambda i, j: (i, 0)),
        ],
        out_shape=[
            jax.ShapeDtypeStruct((T, n), BF16),
            jax.ShapeDtypeStruct((T, LANES), F32),
        ],
        scratch_shapes=[
            pltpu.VMEM((TM, D), BF16),
            pltpu.VMEM((TM, 1), F32),
            pltpu.VMEM((TM, 1), F32),
        ],
        compiler_params=_cp(("arbitrary", "arbitrary")),
        name="gmlp_in",
    )(x, g, sc, sh, w, b.reshape(1, n))


def _gmlp_gate_kernel(u_ref, v_ref, st_ref, lng_ref, lnb_ref, ws_ref, bs_ref, o_ref):
    row = lax.broadcasted_iota(jnp.int32, (CHUNK, CHUNK), 0)
    col = lax.broadcasted_iota(jnp.int32, (CHUNK, CHUNK), 1)
    causal = row >= col
    ws = [jnp.where(causal, ws_ref[g], 0.0).astype(BF16) for g in range(GROUPS)]
    for c in range(TMG // CHUNK):
        rs = slice(c * CHUNK, (c + 1) * CHUNK)
        mean = st_ref[rs, 0:1]
        rstd = st_ref[rs, 1:2]
        for g in range(GROUPS):
            cs = slice(g * GDIM, (g + 1) * GDIM)
            vn = (v_ref[rs, cs].astype(F32) - mean) * rstd * lng_ref[:, cs] + lnb_ref[:, cs]
            mixed = jnp.dot(ws[g], vn.astype(BF16), preferred_element_type=F32) + bs_ref[g]
            o_ref[rs, cs] = (u_ref[rs, cs].astype(F32) * mixed).astype(BF16)


def _gmlp_gate(z, st, ln_g, ln_b, w_s, b_s):
    return pl.pallas_call(
        _gmlp_gate_kernel,
        grid=(T // TMG,),
        in_specs=[
            pl.BlockSpec((TMG, GW), lambda i: (i, 0)),
            pl.BlockSpec((TMG, GW), lambda i: (i, 1)),
            pl.BlockSpec((TMG, LANES), lambda i: (i, 0)),
            pl.BlockSpec((1, GW), lambda i: (0, 0)),
            pl.BlockSpec((1, GW), lambda i: (0, 0)),
            pl.BlockSpec((GROUPS, CHUNK, CHUNK), lambda i: (0, 0, 0)),
            pl.BlockSpec((GROUPS, CHUNK, 1), lambda i: (0, 0, 0)),
        ],
        out_specs=pl.BlockSpec((TMG, GW), lambda i: (i, 0)),
        out_shape=jax.ShapeDtypeStruct((T, GW), BF16),
        compiler_params=_cp(("arbitrary",)),
        name="gmlp_gate",
    )(z, z, st, ln_g.reshape(1, GW), ln_b.reshape(1, GW), w_s, b_s.reshape(GROUPS, CHUNK, 1))


def _res_gemm_kernel(a_ref, w_ref, x_ref, gate_ref, o_ref):
    y = jnp.dot(a_ref[...], w_ref[...], preferred_element_type=F32)
    o_ref[...] = x_ref[...] + gate_ref[0] * y


def _res_gemm(a, w, x, gate, *, tn, name):
    k = a.shape[1]
    return pl.pallas_call(
        _res_gemm_kernel,
        grid=(T // TM, D // tn),
        in_specs=[
            pl.BlockSpec((TM, k), lambda i, j: (i, 0)),
            pl.BlockSpec((k, tn), lambda i, j: (0, j)),
            pl.BlockSpec((TM, tn), lambda i, j: (i, j)),
            pl.BlockSpec((1, 1, tn), lambda i, j: (i * TM // SEQ, 0, j)),
        ],
        out_specs=pl.BlockSpec((TM, tn), lambda i, j: (i, j)),
        out_shape=jax.ShapeDtypeStruct((T, D), F32),
        compiler_params=_cp(("arbitrary", "arbitrary")),
        name=name,
    )(a, w, x, gate)


def _ffn_in_kernel(x_ref, g_ref, sc_ref, sh_ref, wg_ref, wu_ref, o_ref, h_scr):
    @pl.when(pl.program_id(1) == 0)
    def _():
        _norm_prologue(x_ref, g_ref, sc_ref, sh_ref, h_scr)

    h = h_scr[...]
    gate = jnp.dot(h, wg_ref[...], preferred_element_type=F32)
    up = jnp.dot(h, wu_ref[...], preferred_element_type=F32)
    o_ref[...] = (gate * jax.nn.sigmoid(gate) * up).astype(BF16)


def _ffn_in(x, g, sc, sh, w_gu):
    nj = DFF // TN
    return pl.pallas_call(
        _ffn_in_kernel,
        grid=(T // TM, nj),
        in_specs=_norm_specs(TM) + [
            pl.BlockSpec((D, TN), lambda i, j: (0, j)),
            pl.BlockSpec((D, TN), lambda i, j: (0, j + nj)),
        ],
        out_specs=pl.BlockSpec((TM, TN), lambda i, j: (i, j)),
        out_shape=jax.ShapeDtypeStruct((T, DFF), BF16),
        scratch_shapes=[pltpu.VMEM((TM, D), BF16)],
        compiler_params=_cp(("arbitrary", "arbitrary")),
        name="ffn_in",
    )(x, g, sc, sh, w_gu, w_gu)


def _qkv_kernel(x_ref, g_ref, sc_ref, sh_ref, pos_ref, invf_ref, w_ref, gain_ref, flag_ref,
                ind_ref, indt_ref, o_ref, h_scr, ct_scr, s1_scr, s2_scr):
    @pl.when(pl.program_id(1) == 0)
    def _():
        _norm_prologue(x_ref, g_ref, sc_ref, sh_ref, h_scr)
        lane = lax.broadcasted_iota(jnp.int32, ct_scr.shape, 1)
        m = lane & (HD - 1)
        ang = pos_ref[...] * invf_ref[...]
        c = jnp.cos(ang)
        s = jnp.sin(ang)
        ct_scr[...] = jnp.where(m < ROT, c, 1.0)
        s1_scr[...] = jnp.where(m < ROT // 2, -s, 0.0)
        s2_scr[...] = jnp.where((m >= ROT // 2) & (m < ROT), s, 0.0)

    y = jnp.dot(h_scr[...], w_ref[...], preferred_element_type=F32)
    y2 = y * y
    hi = y2.astype(BF16)
    lo = (y2 - hi.astype(F32)).astype(BF16)
    ind = ind_ref[...]
    ss = (jnp.dot(hi, ind, preferred_element_type=F32)
          + jnp.dot(lo, ind, preferred_element_type=F32))
    r = lax.rsqrt(ss * (1.0 / HD) + EPS)
    rhi = r.astype(BF16)
    rlo = (r - rhi.astype(F32)).astype(BF16)
    indt = indt_ref[...]
    rb = (jnp.dot(rhi, indt, preferred_element_type=F32)
          + jnp.dot(rlo, indt, preferred_element_type=F32))
    yn = y * rb * gain_ref[...]
    ct = ct_scr[...]
    s1 = s1_scr[...]
    s2 = s2_scr[...]
    parts = []
    for k in range(TN // LANES):
        yb = yn[:, k * LANES:(k + 1) * LANES]
        parts.append(yb * ct + pltpu.roll(yb, LANES - ROT // 2, 1) * s1
                     + pltpu.roll(yb, ROT // 2, 1) * s2)
    yr = jnp.concatenate(parts, axis=1)
    o_ref[...] = jnp.where(flag_ref[...] > 0.5, yr, y).astype(BF16)


def _qkv(x, g, sc, sh, pos, invf, w, gain, flag, ind, indt):
    return pl.pallas_call(
        _qkv_kernel,
        grid=(T // TM, QKV // TN),
        in_specs=_norm_specs(TM) + [
            pl.BlockSpec((TM, 1), lambda i, j: (i, 0)),
            pl.BlockSpec((1, LANES), lambda i, j: (0, 0)),
            pl.BlockSpec((D, TN), lambda i, j: (0, j)),
            pl.BlockSpec((1, TN), lambda i, j: (0, j)),
            pl.BlockSpec((1, TN), lambda i, j: (0, j)),
            pl.BlockSpec((TN, LANES), lambda i, j: (0, 0)),
            pl.BlockSpec((LANES, TN), lambda i, j: (0, 0)),
        ],
        out_specs=pl.BlockSpec((TM, TN), lambda i, j: (i, j)),
        out_shape=jax.ShapeDtypeStruct((T, QKV), BF16),
        scratch_shapes=[
            pltpu.VMEM((TM, D), BF16),
            pltpu.VMEM((TM, LANES), F32),
            pltpu.VMEM((TM, LANES), F32),
            pltpu.VMEM((TM, LANES), F32),
        ],
        compiler_params=_cp(("arbitrary", "arbitrary")),
        name="qkv",
    )(x, g, sc, sh, pos, invf, w, gain, flag, ind, indt)


def _attn_kernel(q_ref, kp_ref, kc_ref, vp_ref, vc_ref, sink_ref, o_ref):
    n = pl.program_id(1)
    rows = GQA * BLK
    qi = lax.broadcasted_iota(jnp.int32, (rows, 2 * BLK), 0) & (BLK - 1)
    ri = lax.broadcasted_iota(jnp.int32, (rows, 2 * BLK), 1)
    valid = (ri > qi) & (ri <= qi + BLK) & ((ri >= BLK) | (n > 0))
    for kv in range(NKV):
        q = q_ref[0, 0, kv]
        kw = jnp.concatenate([kp_ref[0, 0, kv], kc_ref[0, 0, kv]], axis=0)
        vw = jnp.concatenate([vp_ref[0, 0, kv], vc_ref[0, 0, kv]], axis=0)
        s = lax.dot_general(q, kw, (((1,), (1,)), ((), ())), preferred_element_type=F32)
        s = jnp.where(valid, s, MASK_VALUE)
        sink = sink_ref[kv]
        m = jnp.maximum(jnp.max(s, axis=-1, keepdims=True), sink)
        p = jnp.exp(s - m)
        den = jnp.sum(p, axis=-1, keepdims=True) + jnp.exp(sink - m)
        probs = (p * (1.0 / den)).astype(BF16)
        o_ref[0, 0, kv] = jnp.dot(probs, vw, preferred_element_type=F32).astype(BF16)


def _attention(q5, k5, v5, sink5):
    rows = GQA * BLK
    cur = lambda b, n: (b, n, 0, 0, 0)
    prev = lambda b, n: (b, jnp.maximum(n - 1, 0), 0, 0, 0)
    return pl.pallas_call(
        _attn_kernel,
        grid=(BATCH, NB),
        in_specs=[
            pl.BlockSpec((1, 1, NKV, rows, HD), cur),
            pl.BlockSpec((1, 1, NKV, BLK, HD), prev),
            pl.BlockSpec((1, 1, NKV, BLK, HD), cur),
            pl.BlockSpec((1, 1, NKV, BLK, HD), prev),
            pl.BlockSpec((1, 1, NKV, BLK, HD), cur),
            pl.BlockSpec((NKV, rows, 1), lambda b, n: (0, 0, 0)),
        ],
        out_specs=pl.BlockSpec((1, 1, NKV, rows, HD), cur),
        out_shape=jax.ShapeDtypeStruct((BATCH, NB, NKV, rows, HD), BF16),
        compiler_params=_cp(("arbitrary", "arbitrary")),
        name="attention",
    )(q5, k5, k5, v5, v5, sink5)


def _router_kernel(x_ref, g_ref, sc_ref, sh_ref, rw_ref, rb_ref, route_ref, cnt_ref, run_scr):
    i = pl.program_id(0)

    @pl.when(i == 0)
    def _():
        run_scr[...] = jnp.zeros_like(run_scr)

    h = _norm_rows(x_ref[...], g_ref[...], 1.0 + sc_ref[0], sh_ref[0])

    logits = jnp.dot(h, rw_ref[...], preferred_element_type=F32,
                     precision=lax.Precision.HIGHEST) + rb_ref[...]
    lane = lax.broadcasted_iota(jnp.int32, logits.shape, 1).astype(F32)
    neg = -jnp.inf
    lg = jnp.where(lane < NE, logits, neg)
    v0 = jnp.max(lg, axis=-1, keepdims=True)
    i0 = jnp.min(jnp.where(lg == v0, lane, float(LANES)), axis=-1, keepdims=True)
    lg2 = jnp.where(lane == i0, neg, lg)
    v1 = jnp.max(lg2, axis=-1, keepdims=True)
    i1 = jnp.min(jnp.where(lg2 == v1, lane, float(LANES)), axis=-1, keepdims=True)
    t = jnp.exp(v1 - v0)
    w0 = 1.0 / (1.0 + t)
    w1 = t / (1.0 + t)

    sel0 = lane == i0
    sel1 = lane == i1
    onehot = jnp.where(sel0 | sel1, 1.0, 0.0)
    tm = onehot.shape[0]
    rr = lax.broadcasted_iota(jnp.int32, (tm, tm), 0)
    cc = lax.broadcasted_iota(jnp.int32, (tm, tm), 1)
    lower = jnp.where(rr > cc, 1.0, 0.0).astype(BF16)
    before = jnp.dot(lower, onehot.astype(BF16), preferred_element_type=F32) + run_scr[...]
    rank0 = jnp.sum(jnp.where(sel0, before, 0.0), axis=-1, keepdims=True)
    rank1 = jnp.sum(jnp.where(sel1, before, 0.0), axis=-1, keepdims=True)
    run_scr[...] += jnp.sum(onehot, axis=0, keepdims=True)
    cnt_ref[...] = run_scr[...]

    route_ref[...] = jnp.where(
        lane == 0, i0, jnp.where(
            lane == 1, i1, jnp.where(
                lane == 2, rank0, jnp.where(
                    lane == 3, rank1, jnp.where(
                        lane == 4, w0, jnp.where(lane == 5, w1, 0.0))))))


def _router(x, g, sc, sh, rw_pad, rb_pad):
    return pl.pallas_call(
        _router_kernel,
        grid=(T // TMR,),
        in_specs=[
            pl.BlockSpec((TMR, D), lambda i: (i, 0)),
            pl.BlockSpec((1, D), lambda i: (0, 0)),
            pl.BlockSpec((1, 1, D), lambda i: (i * TMR // SEQ, 0, 0)),
            pl.BlockSpec((1, 1, D), lambda i: (i * TMR // SEQ, 0, 0)),
            pl.BlockSpec((D, LANES), lambda i: (0, 0)),
            pl.BlockSpec((1, LANES), lambda i: (0, 0)),
        ],
        out_specs=[
            pl.BlockSpec((TMR, LANES), lambda i: (i, 0)),
            pl.BlockSpec((1, LANES), lambda i: (0, 0)),
        ],
        out_shape=[
            jax.ShapeDtypeStruct((T, LANES), F32),
            jax.ShapeDtypeStruct((1, LANES), F32),
        ],
        scratch_shapes=[pltpu.VMEM((1, LANES), F32)],
        compiler_params=_cp(("arbitrary",)),
        name="router",
    )(x, g, sc, sh, rw_pad, rb_pad)


def _dispatch_kernel(p0_ref, p1_ref, x_ref, g_ref, sc_ref, sh_ref, xs_in_hbm, xs_hbm, hbuf, sem):
    del xs_in_hbm
    i = pl.program_id(0)
    n = pl.num_programs(0)
    slot = i % 2

    def drain(s):
        for _ in range(2):
            pltpu.make_async_copy(hbuf.at[s], xs_hbm.at[pl.ds(0, TMD)], sem.at[s]).wait()

    @pl.when(i >= 2)
    def _():
        drain(slot)

    hbuf[slot] = _norm_rows(x_ref[...], g_ref[...], 1.0 + sc_ref[0], sh_ref[0])

    def issue(t, carry):
        tok = i * TMD + t
        src = hbuf.at[slot, pl.ds(t, 1)]
        pltpu.make_async_copy(src, xs_hbm.at[pl.ds(p0_ref[tok], 1)], sem.at[slot]).start()
        pltpu.make_async_copy(src, xs_hbm.at[pl.ds(p1_ref[tok], 1)], sem.at[slot]).start()
        return carry
    lax.fori_loop(0, TMD, issue, 0, unroll=8)

    @pl.when(i == n - 1)
    def _():
        drain(1 - slot)
        drain(slot)


def _dispatch(pos0, pos1, x, g, sc, sh, xs_zero):
    return pl.pallas_call(
        _dispatch_kernel,
        grid_spec=pltpu.PrefetchScalarGridSpec(
            num_scalar_prefetch=2,
            grid=(T // TMD,),
            in_specs=[
                pl.BlockSpec((TMD, D), lambda i, p0, p1: (i, 0)),
                pl.BlockSpec((1, D), lambda i, p0, p1: (0, 0)),
                pl.BlockSpec((1, 1, D), lambda i, p0, p1: (i * TMD // SEQ, 0, 0)),
                pl.BlockSpec((1, 1, D), lambda i, p0, p1: (i * TMD // SEQ, 0, 0)),
                pl.BlockSpec(memory_space=pl.ANY),
            ],
            out_specs=pl.BlockSpec(memory_space=pl.ANY),
            scratch_shapes=[pltpu.VMEM((2, TMD, D), F32), pltpu.SemaphoreType.DMA((2,))],
        ),
        out_shape=jax.ShapeDtypeStruct((R, D), F32),
        input_output_aliases={6: 0},
        compiler_params=_cp(("arbitrary",)),
        name="moe_dispatch",
    )(pos0, pos1, x, g, sc, sh, xs_zero)


def _moe_up_kernel(te_ref, first_ref, nu_ref, xs_ref, wg_ref, wu_ref, o_ref, wg_scr, wu_scr):
    i = pl.program_id(1)

    @pl.when(first_ref[i] == 1)
    def _():
        wg_scr[...] = wg_ref[0].astype(BF16)
        wu_scr[...] = wu_ref[0].astype(BF16)

    @pl.when(i < nu_ref[0])
    def _():
        x = xs_ref[...].astype(BF16)
        gate = jnp.dot(x, wg_scr[...], preferred_element_type=F32)
        up = jnp.dot(x, wu_scr[...], preferred_element_type=F32)
        o_ref[...] = (gate * jax.nn.sigmoid(gate) * up).astype(BF16)

    @pl.when(i >= nu_ref[0])
    def _():
        o_ref[...] = jnp.zeros_like(o_ref)


def _moe_up(te, first, nu, xs, w_gu):
    nj = DFE // TNE1
    row = lambda j, i, te, first, nu: (jnp.minimum(i, nu[0] - 1), 0)
    return pl.pallas_call(
        _moe_up_kernel,
        grid_spec=pltpu.PrefetchScalarGridSpec(
            num_scalar_prefetch=3,
            grid=(nj, NT),
            in_specs=[
                pl.BlockSpec((TME, D), row),
                pl.BlockSpec((1, D, TNE1), lambda j, i, te, first, nu: (te[i], 0, j)),
                pl.BlockSpec((1, D, TNE1), lambda j, i, te, first, nu: (te[i], 0, j + nj)),
            ],
            out_specs=pl.BlockSpec((TME, TNE1), lambda j, i, te, first, nu: (i, j)),
            scratch_shapes=[pltpu.VMEM((D, TNE1), BF16), pltpu.VMEM((D, TNE1), BF16)],
        ),
        out_shape=jax.ShapeDtypeStruct((R, DFE), BF16),
        compiler_params=_cp(("arbitrary", "arbitrary")),
        name="moe_up",
    )(te, first, nu, xs, w_gu, w_gu)


def _moe_down_kernel(te_ref, first_ref, nu_ref, a_ref, w_ref, o_ref, w_scr):
    i = pl.program_id(1)

    @pl.when(first_ref[i] == 1)
    def _():
        w_scr[...] = w_ref[0].astype(BF16)

    @pl.when(i < nu_ref[0])
    def _():
        o_ref[...] = jnp.dot(a_ref[...], w_scr[...], preferred_element_type=F32)

    @pl.when(i >= nu_ref[0])
    def _():
        o_ref[...] = jnp.zeros_like(o_ref)


def _moe_down(te, first, nu, a, w_d):
    row = lambda j, i, te, first, nu: (jnp.minimum(i, nu[0] - 1), 0)
    return pl.pallas_call(
        _moe_down_kernel,
        grid_spec=pltpu.PrefetchScalarGridSpec(
            num_scalar_prefetch=3,
            grid=(D // TNE2, NT),
            in_specs=[
                pl.BlockSpec((TME, DFE), row),
                pl.BlockSpec((1, DFE, TNE2), lambda j, i, te, first, nu: (te[i], 0, j)),
            ],
            out_specs=pl.BlockSpec((TME, TNE2), lambda j, i, te, first, nu: (i, j)),
            scratch_shapes=[pltpu.VMEM((DFE, TNE2), BF16)],
        ),
        out_shape=jax.ShapeDtypeStruct((R, D), F32),
        compiler_params=pltpu.CompilerParams(
            dimension_semantics=("arbitrary", "arbitrary"),
            vmem_limit_bytes=60 * 1024 * 1024),
        name="moe_down",
    )(te, first, nu, a, w_d)


def _combine_kernel(p0_ref, p1_ref, ys_hbm, x_ref, route_ref, gate_ref, o_ref, buf, sem):
    i = pl.program_id(0)
    n = pl.num_programs(0)

    def row_copy(src, k, t, slot):
        return pltpu.make_async_copy(ys_hbm.at[pl.ds(src, 1)], buf.at[slot, k, pl.ds(t, 1)],
                                     sem.at[slot])

    def issue(step, slot):
        def body(t, carry):
            tok = step * TMC + t
            row_copy(p0_ref[tok], 0, t, slot).start()
            row_copy(p1_ref[tok], 1, t, slot).start()
            return carry
        lax.fori_loop(0, TMC, body, 0, unroll=8)

    def drain(slot):
        for k in range(2):
            pltpu.make_async_copy(ys_hbm.at[pl.ds(0, TMC)], buf.at[slot, k], sem.at[slot]).wait()

    @pl.when(i == 0)
    def _():
        issue(0, 0)

    @pl.when(i + 1 < n)
    def _():
        issue(i + 1, (i + 1) % 2)

    slot = i % 2
    drain(slot)
    w0 = route_ref[:, 4:5]
    w1 = route_ref[:, 5:6]
    y = w0 * buf[slot, 0] + w1 * buf[slot, 1]
    o_ref[...] = x_ref[...] + gate_ref[0] * y


def _combine(pos0, pos1, ys, x, route, gate):
    return pl.pallas_call(
        _combine_kernel,
        grid_spec=pltpu.PrefetchScalarGridSpec(
            num_scalar_prefetch=2,
            grid=(T // TMC,),
            in_specs=[
                pl.BlockSpec(memory_space=pl.ANY),
                pl.BlockSpec((TMC, D), lambda i, p0, p1: (i, 0)),
                pl.BlockSpec((TMC, LANES), lambda i, p0, p1: (i, 0)),
                pl.BlockSpec((1, 1, D), lambda i, p0, p1: (i * TMC // SEQ, 0, 0)),
            ],
            out_specs=pl.BlockSpec((TMC, D), lambda i, p0, p1: (i, 0)),
            scratch_shapes=[
                pltpu.VMEM((2, 2, TMC, D), F32),
                pltpu.SemaphoreType.DMA((2,)),
            ],
        ),
        out_shape=jax.ShapeDtypeStruct((T, D), F32),
        compiler_params=_cp(("arbitrary",)),
        name="moe_combine",
    )(pos0, pos1, ys, x, route, gate)


def _routing_plan(route, counts):
    e0 = route[:, 0].astype(jnp.int32)
    e1 = route[:, 1].astype(jnp.int32)
    r0 = route[:, 2].astype(jnp.int32)
    r1 = route[:, 3].astype(jnp.int32)
    cnt = counts[0, :NE].astype(jnp.int32)
    ntile = (cnt + TME - 1) // TME
    tile_end = jnp.cumsum(ntile)
    offs = (tile_end - ntile) * TME
    nu = tile_end[-1]
    tid = jnp.arange(NT, dtype=jnp.int32)
    te_raw = jnp.sum((tid[:, None] >= tile_end[None, :]).astype(jnp.int32), axis=1)
    te_last = jnp.sum((nu - 1 >= tile_end).astype(jnp.int32))
    te = jnp.where(tid < nu, te_raw, te_last).astype(jnp.int32)
    prev = jnp.concatenate([jnp.full((1,), -1, jnp.int32), te[:-1]])
    first = ((te != prev) & (tid < nu)).astype(jnp.int32)
    pos0 = offs[e0] + r0
    pos1 = offs[e1] + r1
    return pos0, pos1, te, first, nu.reshape(1).astype(jnp.int32)


def kernel(x, c, positions, ada_w, ada_b, norm_g, gmlp_w_in, gmlp_b_in, gmlp_ln_g, gmlp_ln_b,
           gmlp_w_s, gmlp_b_s, gmlp_w_out, attn_w_qkv, attn_q_norm_g, attn_k_norm_g, attn_sinks,
           attn_w_o, ffn_w_gate_up, ffn_w_down, moe_router_w, moe_router_b, moe_w_gate_up,
           moe_w_down):
    xf = x.reshape(T, D)
    c_pad = jnp.concatenate([c, jnp.zeros((8 - BATCH, D), F32)], axis=0)
    mod = _ada_mod(c_pad, ada_w, ada_b)[:, :BATCH, :]

    def mods(layer):
        return [mod[layer, :, k * D:(k + 1) * D].reshape(BATCH, 1, D) for k in range(6)]

    sh1, sc1, g1, sh2, sc2, g2 = mods(0)
    z, st = _gmlp_in(xf, norm_g[0, 0].reshape(1, D), sc1, sh1, gmlp_w_in[0].astype(BF16),
                     gmlp_b_in[0])
    gated = _gmlp_gate(z, st, gmlp_ln_g[0], gmlp_ln_b[0], gmlp_w_s[0], gmlp_b_s[0])
    xf = _res_gemm(gated, gmlp_w_out[0].astype(BF16), xf, g1, tn=TN, name="gmlp_out")
    act = _ffn_in(xf, norm_g[0, 1].reshape(1, D), sc2, sh2, ffn_w_gate_up[0].astype(BF16))
    xf = _res_gemm(act, ffn_w_down[0].astype(BF16), xf, g2, tn=TN, name="ffn_out")

    sh1, sc1, g1, sh2, sc2, g2 = mods(1)
    pos = positions.reshape(T, 1).astype(F32)
    inv_freq = THETA ** (-jnp.arange(0, ROT, 2, dtype=F32) / ROT)
    invf = jnp.tile(inv_freq, LANES // (ROT // 2)).reshape(1, LANES)
    scale = HD ** -0.5
    gain = jnp.concatenate([jnp.tile(attn_q_norm_g[0] * scale, NH),
                            jnp.tile(attn_k_norm_g[0], NKV),
                            jnp.ones((NKV * HD,), F32)]).reshape(1, QKV)
    flag = jnp.concatenate([jnp.ones(((NH + NKV) * HD,), F32),
                            jnp.zeros((NKV * HD,), F32)]).reshape(1, QKV)
    head_of_col = jnp.arange(TN, dtype=jnp.int32) // HD
    ind = (head_of_col[:, None] == jnp.arange(LANES, dtype=jnp.int32)[None, :]).astype(BF16)
    qkv = _qkv(xf, norm_g[1, 0].reshape(1, D), sc1, sh1, pos, invf, attn_w_qkv[0].astype(BF16),
               gain, flag,
               ind, ind.T)
    q5 = (qkv[:, :NH * HD].reshape(BATCH, NB, BLK, NKV, GQA, HD)
          .transpose(0, 1, 3, 4, 2, 5).reshape(BATCH, NB, NKV, GQA * BLK, HD))
    k5 = (qkv[:, NH * HD:(NH + NKV) * HD].reshape(BATCH, NB, BLK, NKV, HD)
          .transpose(0, 1, 3, 2, 4))
    v5 = (qkv[:, (NH + NKV) * HD:].reshape(BATCH, NB, BLK, NKV, HD)
          .transpose(0, 1, 3, 2, 4))
    sink5 = jnp.repeat(attn_sinks[0].astype(F32), BLK).reshape(NKV, GQA * BLK, 1)
    o5 = _attention(q5, k5, v5, sink5)
    o = (o5.reshape(BATCH, NB, NKV, GQA, BLK, HD).transpose(0, 1, 4, 2, 3, 5)
         .reshape(T, NH * HD))
    xf = _res_gemm(o, attn_w_o[0].astype(BF16), xf, g1, tn=TN, name="attn_out")

    rw_pad = jnp.concatenate([moe_router_w[0].astype(F32),
                              jnp.zeros((D, LANES - NE), F32)], axis=1)
    rb_pad = jnp.concatenate([moe_router_b[0].astype(F32),
                              jnp.zeros((LANES - NE,), F32)]).reshape(1, LANES)
    route, counts = _router(xf, norm_g[1, 1].reshape(1, D), sc2, sh2, rw_pad, rb_pad)
    pos0, pos1, te, first, nu = _routing_plan(route, counts)
    xs = _dispatch(pos0, pos1, xf, norm_g[1, 1].reshape(1, D), sc2, sh2, jnp.zeros((R, D), F32))
    hmid = _moe_up(te, first, nu, xs, moe_w_gate_up[0])
    ys = _moe_down(te, first, nu, hmid, moe_w_down[0])
    xf = _combine(pos0, pos1, ys, xf, route, g2)
    return xf.reshape(BATCH, SEQ, D)
```

```python
import functools
import math

import jax
import jax.numpy as jnp
from jax import lax
from jax.experimental import pallas as pl
from jax.experimental.pallas import tpu as pltpu

F32 = jnp.float32
BF16 = jnp.bfloat16

D = 2048
BATCH = 4
SEQ = 4096
T = BATCH * SEQ
EPS = 1e-6
MASK_VALUE = -1e30

CHUNK = 128
GW = 2 * D
GROUPS = 8
GDIM = GW // GROUPS

HD = 64
NH = D // HD
NKV = 4
GQA = NH // NKV
BLK = 128
NB = SEQ // BLK
ROT = 16
THETA = 500000.0
QKV = (NH + 2 * NKV) * HD

DFF = 5632
NE = 8
DFE = 7168

LANES = 128
VMEM_LIMIT = 56 * 1024 * 1024

TM = 1024
TN = 512
TMG = 512
TMR = 512
TME = 512
TNE1 = 512
TNE2 = 512
TMC = 256
TMD = 256
R = 2 * T + NE * TME
NT = R // TME


def _cp(sem):
    return pltpu.CompilerParams(dimension_semantics=sem, vmem_limit_bytes=VMEM_LIMIT)


def _ada_kernel(c_ref, w_ref, b_ref, o_ref):
    c = c_ref[...]
    sc = (c * jax.nn.sigmoid(c)).astype(BF16)
    o_ref[0] = jnp.dot(sc, w_ref[0].astype(BF16), preferred_element_type=F32) + b_ref[0]


def _ada_mod(c_pad, ada_w, ada_b):
    depth = ada_w.shape[0]
    n = ada_w.shape[2]
    tn = 1024
    return pl.pallas_call(
        _ada_kernel,
        grid=(depth, n // tn),
        in_specs=[
            pl.BlockSpec((8, D), lambda l, j: (0, 0)),
            pl.BlockSpec((1, D, tn), lambda l, j: (l, 0, j)),
            pl.BlockSpec((1, 1, tn), lambda l, j: (l, 0, j)),
        ],
        out_specs=pl.BlockSpec((1, 8, tn), lambda l, j: (l, 0, j)),
        out_shape=jax.ShapeDtypeStruct((depth, 8, n), F32),
        compiler_params=_cp(("arbitrary", "arbitrary")),
        name="ada_mod",
    )(c_pad, ada_w, ada_b.reshape(depth, 1, n))


def _norm_rows(x, g, a, b):
    ms = jnp.mean(x * x, axis=-1, keepdims=True)
    return (x * lax.rsqrt(ms + EPS) * g) * a + b


def _norm_prologue(x_ref, g_ref, sc_ref, sh_ref, h_ref, rows=256):
    g = g_ref[...]
    a = 1.0 + sc_ref[0]
    b = sh_ref[0]
    for r0 in range(0, x_ref.shape[0], rows):
        h_ref[r0:r0 + rows, :] = _norm_rows(x_ref[r0:r0 + rows, :], g, a, b).astype(BF16)


def _norm_specs(tm):
    return [
        pl.BlockSpec((tm, D), lambda i, j: (i, 0)),
        pl.BlockSpec((1, D), lambda i, j: (0, 0)),
        pl.BlockSpec((1, 1, D), lambda i, j: (i * tm // SEQ, 0, 0)),
        pl.BlockSpec((1, 1, D), lambda i, j: (i * tm // SEQ, 0, 0)),
    ]


def _gmlp_in_kernel(x_ref, g_ref, sc_ref, sh_ref, w_ref, b_ref, z_ref, st_ref,
                    h_scr, s1_scr, s2_scr, *, nj, jv0, nv):
    j = pl.program_id(1)

    @pl.when(j == 0)
    def _():
        _norm_prologue(x_ref, g_ref, sc_ref, sh_ref, h_scr)
        s1_scr[...] = jnp.zeros_like(s1_scr)
        s2_scr[...] = jnp.zeros_like(s2_scr)

    acc = jnp.dot(h_scr[...], w_ref[...], preferred_element_type=F32) + b_ref[...]
    z = 0.5 * acc * (1.0 + lax.erf(acc * (1.0 / math.sqrt(2.0))))
    z_ref[...] = z.astype(BF16)

    @pl.when(j >= jv0)
    def _():
        s1_scr[...] += jnp.sum(z, axis=-1, keepdims=True)
        s2_scr[...] += jnp.sum(z * z, axis=-1, keepdims=True)

    @pl.when(j == nj - 1)
    def _():
        mean = s1_scr[...] * (1.0 / nv)
        var = s2_scr[...] * (1.0 / nv) - mean * mean
        rstd = lax.rsqrt(var + EPS)
        lane = lax.broadcasted_iota(jnp.int32, st_ref.shape, 1)
        st_ref[...] = jnp.where(lane == 0, mean, jnp.where(lane == 1, rstd, 0.0))


def _gmlp_in(x, g, sc, sh, w, b):
    n = w.shape[1]
    nj = n // TN
    kern = functools.partial(_gmlp_in_kernel, nj=nj, jv0=GW // TN, nv=float(GW))
    return pl.pallas_call(
        kern,
        grid=(T // TM, nj),
        in_specs=_norm_specs(TM) + [
            pl.BlockSpec((D, TN), lambda i, j: (0, j)),
            pl.BlockSpec((1, TN), lambda i, j: (0, j)),
        ],
        out_specs=[
            pl.BlockSpec((TM, TN), lambda i, j: (i, j)),
            pl.BlockSpec((TM, LANES), lambda i, j: (i, 0)),
        ],
        out_shape=[
            jax.ShapeDtypeStruct((T, n), BF16),
            jax.ShapeDtypeStruct((T, LANES), F32),
        ],
        scratch_shapes=[
            pltpu.VMEM((TM, D), BF16),
            pltpu.VMEM((TM, 1), F32),
            pltpu.VMEM((TM, 1), F32),
        ],
        compiler_params=_cp(("arbitrary", "arbitrary")),
        name="gmlp_in",
    )(x, g, sc, sh, w, b.reshape(1, n))


def _gmlp_gate_kernel(u_ref, v_ref, st_ref, lng_ref, lnb_ref, ws_ref, bs_ref, o_ref):
    row = lax.broadcasted_iota(jnp.int32, (CHUNK, CHUNK), 0)
    col = lax.broadcasted_iota(jnp.int32, (CHUNK, CHUNK), 1)
    causal = row >= col
    ws = [jnp.where(causal, ws_ref[g], 0.0).astype(BF16) for g in range(GROUPS)]
    for c in range(TMG // CHUNK):
        rs = slice(c * CHUNK, (c + 1) * CHUNK)
        mean = st_ref[rs, 0:1]
        rstd = st_ref[rs, 1:2]
        for g in range(GROUPS):
            cs = slice(g * GDIM, (g + 1) * GDIM)
            vn = (v_ref[rs, cs].astype(F32) - mean) * rstd * lng_ref[:, cs] + lnb_ref[:, cs]
            mixed = jnp.dot(ws[g], vn.astype(BF16), preferred_element_type=F32) + bs_ref[g]
            o_ref[rs, cs] = (u_ref[rs, cs].astype(F32) * mixed).astype(BF16)


def _gmlp_gate(z, st, ln_g, ln_b, w_s, b_s):
    return pl.pallas_call(
        _gmlp_gate_kernel,
        grid=(T // TMG,),
        in_specs=[
            pl.BlockSpec((TMG, GW), lambda i: (i, 0)),
            pl.BlockSpec((TMG, GW), lambda i: (i, 1)),
            pl.BlockSpec((TMG, LANES), lambda i: (i, 0)),
            pl.BlockSpec((1, GW), lambda i: (0, 0)),
            pl.BlockSpec((1, GW), lambda i: (0, 0)),
            pl.BlockSpec((GROUPS, CHUNK, CHUNK), lambda i: (0, 0, 0)),
            pl.BlockSpec((GROUPS, CHUNK, 1), lambda i: (0, 0, 0)),
        ],
        out_specs=pl.BlockSpec((TMG, GW), lambda i: (i, 0)),
        out_shape=jax.ShapeDtypeStruct((T, GW), BF16),
        compiler_params=_cp(("arbitrary",)),
        name="gmlp_gate",
    )(z, z, st, ln_g.reshape(1, GW), ln_b.reshape(1, GW), w_s, b_s.reshape(GROUPS, CHUNK, 1))


def _res_gemm_kernel(a_ref, w_ref, x_ref, gate_ref, o_ref):
    y = jnp.dot(a_ref[...], w_ref[...], preferred_element_type=F32)
    o_ref[...] = x_ref[...] + gate_ref[0] * y


def _res_gemm(a, w, x, gate, *, tn, name):
    k = a.shape[1]
    return pl.pallas_call(
        _res_gemm_kernel,
        grid=(T // TM, D // tn),
        in_specs=[
            pl.BlockSpec((TM, k), lambda i, j: (i, 0)),
            pl.BlockSpec((k, tn), lambda i, j: (0, j)),
            pl.BlockSpec((TM, tn), lambda i, j: (i, j)),
            pl.BlockSpec((1, 1, tn), lambda i, j: (i * TM // SEQ, 0, j)),
        ],
        out_specs=pl.BlockSpec((TM, tn), lambda i, j: (i, j)),
        out_shape=jax.ShapeDtypeStruct((T, D), F32),
        compiler_params=_cp(("arbitrary", "arbitrary")),
        name=name,
    )(a, w, x, gate)


def _ffn_in_kernel(x_ref, g_ref, sc_ref, sh_ref, wg_ref, wu_ref, o_ref, h_scr):
    @pl.when(pl.program_id(1) == 0)
    def _():
        _norm_prologue(x_ref, g_ref, sc_ref, sh_ref, h_scr)

    h = h_scr[...]
    gate = jnp.dot(h, wg_ref[...], preferred_element_type=F32)
    up = jnp.dot(h, wu_ref[...], preferred_element_type=F32)
    o_ref[...] = (gate * jax.nn.sigmoid(gate) * up).astype(BF16)


def _ffn_in(x, g, sc, sh, w_gu):
    nj = DFF // TN
    return pl.pallas_call(
        _ffn_in_kernel,
        grid=(T // TM, nj),
        in_specs=_norm_specs(TM) + [
            pl.BlockSpec((D, TN), lambda i, j: (0, j)),
            pl.BlockSpec((D, TN), lambda i, j: (0, j + nj)),
        ],
        out_specs=pl.BlockSpec((TM, TN), lambda i, j: (i, j)),
        out_shape=jax.ShapeDtypeStruct((T, DFF), BF16),
        scratch_shapes=[pltpu.VMEM((TM, D), BF16)],
        compiler_params=_cp(("arbitrary", "arbitrary")),
        name="ffn_in",
    )(x, g, sc, sh, w_gu, w_gu)


def _qkv_kernel(x_ref, g_ref, sc_ref, sh_ref, pos_ref, invf_ref, w_ref, gain_ref, flag_ref,
                ind_ref, indt_ref, q_ref, k_ref, v_ref, h_scr, ct_scr, s1_scr, s2_scr):
    @pl.when(pl.program_id(1) == 0)
    def _():
        _norm_prologue(x_ref, g_ref, sc_ref, sh_ref, h_scr)
        lane = lax.broadcasted_iota(jnp.int32, ct_scr.shape, 1)
        m = lane & (HD - 1)
        ang = pos_ref[...] * invf_ref[...]
        c = jnp.cos(ang)
        s = jnp.sin(ang)
        ct_scr[...] = jnp.where(m < ROT, c, 1.0)
        s1_scr[...] = jnp.where(m < ROT // 2, -s, 0.0)
        s2_scr[...] = jnp.where((m >= ROT // 2) & (m < ROT), s, 0.0)

    y = jnp.dot(h_scr[...], w_ref[...], preferred_element_type=F32)
    y2 = y * y
    hi = y2.astype(BF16)
    lo = (y2 - hi.astype(F32)).astype(BF16)
    ind = ind_ref[...]
    ss = (jnp.dot(hi, ind, preferred_element_type=F32)
          + jnp.dot(lo, ind, preferred_element_type=F32))
    r = lax.rsqrt(ss * (1.0 / HD) + EPS)
    rhi = r.astype(BF16)
    rlo = (r - rhi.astype(F32)).astype(BF16)
    indt = indt_ref[...]
    rb = (jnp.dot(rhi, indt, preferred_element_type=F32)
          + jnp.dot(rlo, indt, preferred_element_type=F32))
    yn = y * rb * gain_ref[...]
    ct = ct_scr[...]
    s1 = s1_scr[...]
    s2 = s2_scr[...]
    parts = []
    for k in range(TN // LANES):
        yb = yn[:, k * LANES:(k + 1) * LANES]
        parts.append(yb * ct + pltpu.roll(yb, LANES - ROT // 2, 1) * s1
                     + pltpu.roll(yb, ROT // 2, 1) * s2)
    yr = jnp.concatenate(parts, axis=1)
    res = jnp.where(flag_ref[...] > 0.5, yr, y).astype(BF16)

    j = pl.program_id(1)

    @pl.when(j < NKV)
    def _():
        for blk in range(TM // BLK):
            rs = slice(blk * BLK, (blk + 1) * BLK)
            for gq in range(GQA):
                q_ref[blk, 0, gq * BLK:(gq + 1) * BLK, :] = res[rs, gq * HD:(gq + 1) * HD]

    @pl.when(j == NKV)
    def _():
        for blk in range(TM // BLK):
            rs = slice(blk * BLK, (blk + 1) * BLK)
            for kv in range(NKV):
                k_ref[blk, kv] = res[rs, kv * HD:(kv + 1) * HD]
                v_ref[blk, kv] = res[rs, (NKV + kv) * HD:(NKV + kv + 1) * HD]


def _qkv(x, g, sc, sh, pos, invf, w, gain, flag, ind, indt):
    nblk = TM // BLK
    return pl.pallas_call(
        _qkv_kernel,
        grid=(T // TM, QKV // TN),
        in_specs=_norm_specs(TM) + [
            pl.BlockSpec((TM, 1), lambda i, j: (i, 0)),
            pl.BlockSpec((1, LANES), lambda i, j: (0, 0)),
            pl.BlockSpec((D, TN), lambda i, j: (0, j)),
            pl.BlockSpec((1, TN), lambda i, j: (0, j)),
            pl.BlockSpec((1, TN), lambda i, j: (0, j)),
            pl.BlockSpec((TN, LANES), lambda i, j: (0, 0)),
            pl.BlockSpec((LANES, TN), lambda i, j: (0, 0)),
        ],
        out_specs=[
            pl.BlockSpec((nblk, 1, GQA * BLK, HD), lambda i, j: (i, jnp.minimum(j, NKV - 1), 0, 0)),
            pl.BlockSpec((nblk, NKV, BLK, HD), lambda i, j: (i, 0, 0, 0)),
            pl.BlockSpec((nblk, NKV, BLK, HD), lambda i, j: (i, 0, 0, 0)),
        ],
        out_shape=[
            jax.ShapeDtypeStruct((BATCH * NB, NKV, GQA * BLK, HD), BF16),
            jax.ShapeDtypeStruct((BATCH * NB, NKV, BLK, HD), BF16),
            jax.ShapeDtypeStruct((BATCH * NB, NKV, BLK, HD), BF16),
        ],
        scratch_shapes=[
            pltpu.VMEM((TM, D), BF16),
            pltpu.VMEM((TM, LANES), F32),
            pltpu.VMEM((TM, LANES), F32),
            pltpu.VMEM((TM, LANES), F32),
        ],
        compiler_params=_cp(("arbitrary", "arbitrary")),
        name="qkv",
    )(x, g, sc, sh, pos, invf, w, gain, flag, ind, indt)


def _attn_kernel(q_ref, kp_ref, kc_ref, vp_ref, vc_ref, sink_ref, o_ref):
    n = pl.program_id(1)
    rows = GQA * BLK
    qi = lax.broadcasted_iota(jnp.int32, (rows, 2 * BLK), 0) & (BLK - 1)
    ri = lax.broadcasted_iota(jnp.int32, (rows, 2 * BLK), 1)
    valid = (ri > qi) & (ri <= qi + BLK) & ((ri >= BLK) | (n > 0))
    for kv in range(NKV):
        q = q_ref[0, kv]
        kw = jnp.concatenate([kp_ref[0, kv], kc_ref[0, kv]], axis=0)
        vw = jnp.concatenate([vp_ref[0, kv], vc_ref[0, kv]], axis=0)
        s = lax.dot_general(q, kw, (((1,), (1,)), ((), ())), preferred_element_type=F32)
        s = jnp.where(valid, s, MASK_VALUE)
        sink = sink_ref[kv]
        m = jnp.maximum(jnp.max(s, axis=-1, keepdims=True), sink)
        p = jnp.exp(s - m)
        den = jnp.sum(p, axis=-1, keepdims=True) + jnp.exp(sink - m)
        probs = (p * (1.0 / den)).astype(BF16)
        o = jnp.dot(probs, vw, preferred_element_type=F32).astype(BF16)
        for gq in range(GQA):
            head = kv * GQA + gq
            o_ref[:, head * HD:(head + 1) * HD] = o[gq * BLK:(gq + 1) * BLK, :]


def _attention(q5, k5, v5, sink5):
    cur = lambda b, n: (b * NB + n, 0, 0, 0)
    prev = lambda b, n: (b * NB + jnp.maximum(n - 1, 0), 0, 0, 0)
    return pl.pallas_call(
        _attn_kernel,
        grid=(BATCH, NB),
        in_specs=[
            pl.BlockSpec((1, NKV, GQA * BLK, HD), cur),
            pl.BlockSpec((1, NKV, BLK, HD), prev),
            pl.BlockSpec((1, NKV, BLK, HD), cur),
            pl.BlockSpec((1, NKV, BLK, HD), prev),
            pl.BlockSpec((1, NKV, BLK, HD), cur),
            pl.BlockSpec((NKV, GQA * BLK, 1), lambda b, n: (0, 0, 0)),
        ],
        out_specs=pl.BlockSpec((BLK, NH * HD), lambda b, n: (b * NB + n, 0)),
        out_shape=jax.ShapeDtypeStruct((T, NH * HD), BF16),
        compiler_params=_cp(("arbitrary", "arbitrary")),
        name="attention",
    )(q5, k5, k5, v5, v5, sink5)


def _router_kernel(x_ref, g_ref, sc_ref, sh_ref, rw_ref, rb_ref, route_ref, cnt_ref, run_scr):
    i = pl.program_id(0)

    @pl.when(i == 0)
    def _():
        run_scr[...] = jnp.zeros_like(run_scr)

    h = _norm_rows(x_ref[...], g_ref[...], 1.0 + sc_ref[0], sh_ref[0])

    logits = jnp.dot(h, rw_ref[...], preferred_element_type=F32,
                     precision=lax.Precision.HIGHEST) + rb_ref[...]
    lane = lax.broadcasted_iota(jnp.int32, logits.shape, 1).astype(F32)
    neg = -jnp.inf
    lg = jnp.where(lane < NE, logits, neg)
    v0 = jnp.max(lg, axis=-1, keepdims=True)
    i0 = jnp.min(jnp.where(lg == v0, lane, float(LANES)), axis=-1, keepdims=True)
    lg2 = jnp.where(lane == i0, neg, lg)
    v1 = jnp.max(lg2, axis=-1, keepdims=True)
    i1 = jnp.min(jnp.where(lg2 == v1, lane, float(LANES)), axis=-1, keepdims=True)
    t = jnp.exp(v1 - v0)
    w0 = 1.0 / (1.0 + t)
    w1 = t / (1.0 + t)

    sel0 = lane == i0
    sel1 = lane == i1
    onehot = jnp.where(sel0 | sel1, 1.0, 0.0)
    tm = onehot.shape[0]
    rr = lax.broadcasted_iota(jnp.int32, (tm, tm), 0)
    cc = lax.broadcasted_iota(jnp.int32, (tm, tm), 1)
    lower = jnp.where(rr > cc, 1.0, 0.0).astype(BF16)
    before = jnp.dot(lower, onehot.astype(BF16), preferred_element_type=F32) + run_scr[...]
    rank0 = jnp.sum(jnp.where(sel0, before, 0.0), axis=-1, keepdims=True)
    rank1 = jnp.sum(jnp.where(sel1, before, 0.0), axis=-1, keepdims=True)
    run_scr[...] += jnp.sum(onehot, axis=0, keepdims=True)
    cnt_ref[...] = run_scr[...]

    route_ref[...] = jnp.where(
        lane == 0, i0, jnp.where(
            lane == 1, i1, jnp.where(
                lane == 2, rank0, jnp.where(
                    lane == 3, rank1, jnp.where(
                        lane == 4, w0, jnp.where(lane == 5, w1, 0.0))))))


def _router(x, g, sc, sh, rw_pad, rb_pad):
    return pl.pallas_call(
        _router_kernel,
        grid=(T // TMR,),
        in_specs=[
            pl.BlockSpec((TMR, D), lambda i: (i, 0)),
            pl.BlockSpec((1, D), lambda i: (0, 0)),
            pl.BlockSpec((1, 1, D), lambda i: (i * TMR // SEQ, 0, 0)),
            pl.BlockSpec((1, 1, D), lambda i: (i * TMR // SEQ, 0, 0)),
            pl.BlockSpec((D, LANES), lambda i: (0, 0)),
            pl.BlockSpec((1, LANES), lambda i: (0, 0)),
        ],
        out_specs=[
            pl.BlockSpec((TMR, LANES), lambda i: (i, 0)),
            pl.BlockSpec((1, LANES), lambda i: (0, 0)),
        ],
        out_shape=[
            jax.ShapeDtypeStruct((T, LANES), F32),
            jax.ShapeDtypeStruct((1, LANES), F32),
        ],
        scratch_shapes=[pltpu.VMEM((1, LANES), F32)],
        compiler_params=_cp(("arbitrary",)),
        name="router",
    )(x, g, sc, sh, rw_pad, rb_pad)


def _dispatch_kernel(p0_ref, p1_ref, x_ref, g_ref, sc_ref, sh_ref, xs_in_hbm, xs_hbm, hbuf, sem):
    del xs_in_hbm
    i = pl.program_id(0)
    n = pl.num_programs(0)
    slot = i % 2

    def drain(s):
        for _ in range(2):
            pltpu.make_async_copy(hbuf.at[s], xs_hbm.at[pl.ds(0, TMD)], sem.at[s]).wait()

    @pl.when(i >= 2)
    def _():
        drain(slot)

    hbuf[slot] = _norm_rows(x_ref[...], g_ref[...], 1.0 + sc_ref[0], sh_ref[0])

    def issue(t, carry):
        tok = i * TMD + t
        src = hbuf.at[slot, pl.ds(t, 1)]
        pltpu.make_async_copy(src, xs_hbm.at[pl.ds(p0_ref[tok], 1)], sem.at[slot]).start()
        pltpu.make_async_copy(src, xs_hbm.at[pl.ds(p1_ref[tok], 1)], sem.at[slot]).start()
        return carry
    lax.fori_loop(0, TMD, issue, 0, unroll=8)

    @pl.when(i == n - 1)
    def _():
        drain(1 - slot)
        drain(slot)


def _dispatch(pos0, pos1, x, g, sc, sh, xs_zero):
    return pl.pallas_call(
        _dispatch_kernel,
        grid_spec=pltpu.PrefetchScalarGridSpec(
            num_scalar_prefetch=2,
            grid=(T // TMD,),
            in_specs=[
                pl.BlockSpec((TMD, D), lambda i, p0, p1: (i, 0)),
                pl.BlockSpec((1, D), lambda i, p0, p1: (0, 0)),
                pl.BlockSpec((1, 1, D), lambda i, p0, p1: (i * TMD // SEQ, 0, 0)),
                pl.BlockSpec((1, 1, D), lambda i, p0, p1: (i * TMD // SEQ, 0, 0)),
                pl.BlockSpec(memory_space=pl.ANY),
            ],
            out_specs=pl.BlockSpec(memory_space=pl.ANY),
            scratch_shapes=[pltpu.VMEM((2, TMD, D), F32), pltpu.SemaphoreType.DMA((2,))],
        ),
        out_shape=jax.ShapeDtypeStruct((R, D), F32),
        input_output_aliases={6: 0},
        compiler_params=_cp(("arbitrary",)),
        name="moe_dispatch",
    )(pos0, pos1, x, g, sc, sh, xs_zero)


def _moe_up_kernel(te_ref, first_ref, nu_ref, xs_ref, wg_ref, wu_ref, o_ref, wg_scr, wu_scr):
    i = pl.program_id(1)

    @pl.when(first_ref[i] == 1)
    def _():
        wg_scr[...] = wg_ref[0].astype(BF16)
        wu_scr[...] = wu_ref[0].astype(BF16)

    @pl.when(i < nu_ref[0])
    def _():
        x = xs_ref[...]
        gate = jnp.dot(x, wg_scr[...], preferred_element_type=F32)
        up = jnp.dot(x, wu_scr[...], preferred_element_type=F32)
        o_ref[...] = (gate * jax.nn.sigmoid(gate) * up).astype(BF16)

    @pl.when(i >= nu_ref[0])
    def _():
        o_ref[...] = jnp.zeros_like(o_ref)


def _moe_up(te, first, nu, xs, w_gu):
    nj = DFE // TNE1
    row = lambda j, i, te, first, nu: (jnp.maximum(jnp.minimum(i, nu[0] - 1), 0), 0)
    return pl.pallas_call(
        _moe_up_kernel,
        grid_spec=pltpu.PrefetchScalarGridSpec(
            num_scalar_prefetch=3,
            grid=(nj, NT),
            in_specs=[
                pl.BlockSpec((TME, D), row),
                pl.BlockSpec((1, D, TNE1), lambda j, i, te, first, nu: (te[i], 0, j)),
                pl.BlockSpec((1, D, TNE1), lambda j, i, te, first, nu: (te[i], 0, j + nj)),
            ],
            out_specs=pl.BlockSpec((TME, TNE1), lambda j, i, te, first, nu: (i, j)),
            scratch_shapes=[pltpu.VMEM((D, TNE1), BF16), pltpu.VMEM((D, TNE1), BF16)],
        ),
        out_shape=jax.ShapeDtypeStruct((R, DFE), BF16),
        compiler_params=_cp(("arbitrary", "arbitrary")),
        name="moe_up",
    )(te, first, nu, xs, w_gu, w_gu)


def _moe_down_kernel(te_ref, first_ref, nu_ref, a_ref, w_ref, o_ref, w_scr):
    i = pl.program_id(1)

    @pl.when(first_ref[i] == 1)
    def _():
        w_scr[...] = w_ref[0].astype(BF16)

    @pl.when(i < nu_ref[0])
    def _():
        o_ref[...] = jnp.dot(a_ref[...], w_scr[...], preferred_element_type=F32)

    @pl.when(i >= nu_ref[0])
    def _():
        o_ref[...] = jnp.zeros_like(o_ref)


def _moe_down(te, first, nu, a, w_d):
    row = lambda j, i, te, first, nu: (jnp.maximum(jnp.minimum(i, nu[0] - 1), 0), 0)
    return pl.pallas_call(
        _moe_down_kernel,
        grid_spec=pltpu.PrefetchScalarGridSpec(
            num_scalar_prefetch=3,
            grid=(D // TNE2, NT),
            in_specs=[
                pl.BlockSpec((TME, DFE), row),
                pl.BlockSpec((1, DFE, TNE2), lambda j, i, te, first, nu: (te[i], 0, j)),
            ],
            out_specs=pl.BlockSpec((TME, TNE2), lambda j, i, te, first, nu: (i, j)),
            scratch_shapes=[pltpu.VMEM((DFE, TNE2), BF16)],
        ),
        out_shape=jax.ShapeDtypeStruct((R, D), F32),
        compiler_params=pltpu.CompilerParams(
            dimension_semantics=("arbitrary", "arbitrary"),
            vmem_limit_bytes=60 * 1024 * 1024),
        name="moe_down",
    )(te, first, nu, a, w_d)


def _combine_kernel(p0_ref, p1_ref, ys_hbm, x_ref, route_ref, gate_ref, o_ref, buf, sem):
    i = pl.program_id(0)
    n = pl.num_programs(0)

    def row_copy(src, k, t, slot):
        return pltpu.make_async_copy(ys_hbm.at[pl.ds(src, 1)], buf.at[slot, k, pl.ds(t, 1)],
                                     sem.at[slot])

    def issue(step, slot):
        def body(t, carry):
            tok = step * TMC + t
            row_copy(p0_ref[tok], 0, t, slot).start()
            row_copy(p1_ref[tok], 1, t, slot).start()
            return carry
        lax.fori_loop(0, TMC, body, 0, unroll=8)

    def drain(slot):
        for k in range(2):
            pltpu.make_async_copy(ys_hbm.at[pl.ds(0, TMC)], buf.at[slot, k], sem.at[slot]).wait()

    @pl.when(i == 0)
    def _():
        issue(0, 0)

    @pl.when(i + 1 < n)
    def _():
        issue(i + 1, (i + 1) % 2)

    slot = i % 2
    drain(slot)
    w0 = route_ref[:, 4:5]
    w1 = route_ref[:, 5:6]
    y = w0 * buf[slot, 0] + w1 * buf[slot, 1]
    o_ref[...] = x_ref[...] + gate_ref[0] * y


def _combine(pos0, pos1, ys, x, route, gate):
    return pl.pallas_call(
        _combine_kernel,
        grid_spec=pltpu.PrefetchScalarGridSpec(
            num_scalar_prefetch=2,
            grid=(T // TMC,),
            in_specs=[
                pl.BlockSpec(memory_space=pl.ANY),
                pl.BlockSpec((TMC, D), lambda i, p0, p1: (i, 0)),
                pl.BlockSpec((TMC, LANES), lambda i, p0, p1: (i, 0)),
                pl.BlockSpec((1, 1, D), lambda i, p0, p1: (i * TMC // SEQ, 0, 0)),
            ],
            out_specs=pl.BlockSpec((TMC, D), lambda i, p0, p1: (i, 0)),
            scratch_shapes=[
                pltpu.VMEM((2, 2, TMC, D), F32),
                pltpu.SemaphoreType.DMA((2,)),
            ],
        ),
        out_shape=jax.ShapeDtypeStruct((T, D), F32),
        compiler_params=_cp(("arbitrary",)),
        name="moe_combine",
    )(pos0, pos1, ys, x, route, gate)


def _routing_plan(route, counts):
    e0 = route[:, 0].astype(jnp.int32)
    e1 = route[:, 1].astype(jnp.int32)
    r0 = route[:, 2].astype(jnp.int32)
    r1 = route[:, 3].astype(jnp.int32)
    cnt = counts[0, :NE].astype(jnp.int32)
    ntile = (cnt + TME - 1) // TME
    tile_end = jnp.cumsum(ntile)
    offs = (tile_end - ntile) * TME
    nu = tile_end[-1]
    tid = jnp.arange(NT, dtype=jnp.int32)
    te_raw = jnp.sum((tid[:, None] >= tile_end[None, :]).astype(jnp.int32), axis=1)
    te_last = jnp.sum((nu - 1 >= tile_end).astype(jnp.int32))
    te = jnp.where(tid < nu, te_raw, te_last).astype(jnp.int32)
    prev = jnp.concatenate([jnp.full((1,), -1, jnp.int32), te[:-1]])
    first = ((te != prev) & (tid < nu)).astype(jnp.int32)
    pos0 = offs[e0] + r0
    pos1 = offs[e1] + r1
    return pos0, pos1, te, first, nu.reshape(1).astype(jnp.int32)


def kernel(x, c, positions, ada_w, ada_b, norm_g, gmlp_w_in, gmlp_b_in, gmlp_ln_g, gmlp_ln_b,
           gmlp_w_s, gmlp_b_s, gmlp_w_out, attn_w_qkv, attn_q_norm_g, attn_k_norm_g, attn_sinks,
           attn_w_o, ffn_w_gate_up, ffn_w_down, moe_router_w, moe_router_b, moe_w_gate_up,
           moe_w_down):
    xf = x.reshape(T, D)
    c_pad = jnp.concatenate([c, jnp.zeros((8 - BATCH, D), F32)], axis=0)
    mod = _ada_mod(c_pad, ada_w, ada_b)[:, :BATCH, :]

    def mods(layer):
        return [mod[layer, :, k * D:(k + 1) * D].reshape(BATCH, 1, D) for k in range(6)]

    sh1, sc1, g1, sh2, sc2, g2 = mods(0)
    z, st = _gmlp_in(xf, norm_g[0, 0].reshape(1, D), sc1, sh1, gmlp_w_in[0].astype(BF16),
                     gmlp_b_in[0])
    gated = _gmlp_gate(z, st, gmlp_ln_g[0], gmlp_ln_b[0], gmlp_w_s[0], gmlp_b_s[0])
    xf = _res_gemm(gated, gmlp_w_out[0].astype(BF16), xf, g1, tn=TN, name="gmlp_out")
    act = _ffn_in(xf, norm_g[0, 1].reshape(1, D), sc2, sh2, ffn_w_gate_up[0].astype(BF16))
    xf = _res_gemm(act, ffn_w_down[0].astype(BF16), xf, g2, tn=TN, name="ffn_out")

    sh1, sc1, g1, sh2, sc2, g2 = mods(1)
    pos = positions.reshape(T, 1).astype(F32)
    inv_freq = THETA ** (-jnp.arange(0, ROT, 2, dtype=F32) / ROT)
    invf = jnp.tile(inv_freq, LANES // (ROT // 2)).reshape(1, LANES)
    scale = HD ** -0.5
    gain = jnp.concatenate([jnp.tile(attn_q_norm_g[0] * scale, NH),
                            jnp.tile(attn_k_norm_g[0], NKV),
                            jnp.ones((NKV * HD,), F32)]).reshape(1, QKV)
    flag = jnp.concatenate([jnp.ones(((NH + NKV) * HD,), F32),
                            jnp.zeros((NKV * HD,), F32)]).reshape(1, QKV)
    head_of_col = jnp.arange(TN, dtype=jnp.int32) // HD
    ind = (head_of_col[:, None] == jnp.arange(LANES, dtype=jnp.int32)[None, :]).astype(BF16)
    q5, k5, v5 = _qkv(xf, norm_g[1, 0].reshape(1, D), sc1, sh1, pos, invf,
                      attn_w_qkv[0].astype(BF16), gain, flag, ind, ind.T)
    sink5 = jnp.repeat(attn_sinks[0].astype(F32), BLK).reshape(NKV, GQA * BLK, 1)
    o = _attention(q5, k5, v5, sink5)
    xf = _res_gemm(o, attn_w_o[0].astype(BF16), xf, g1, tn=TN, name="attn_out")

    rw_pad = jnp.concatenate([moe_router_w[0].astype(F32),
                              jnp.zeros((D, LANES - NE), F32)], axis=1)
    rb_pad = jnp.concatenate([moe_router_b[0].astype(F32),
                              jnp.zeros((LANES - NE,), F32)]).reshape(1, LANES)
    route, counts = _router(xf, norm_g[1, 1].reshape(1, D), sc2, sh2, rw_pad, rb_pad)
    pos0, pos1, te, first, nu = _routing_plan(route, counts)
    xs = _dispatch(pos0, pos1, xf, norm_g[1, 1].reshape(1, D), sc2, sh2, jnp.zeros((R, D), F32))
    hmid = _moe_up(te, first, nu, xs.astype(BF16), moe_w_gate_up[0])
    ys = _moe_down(te, first, nu, hmid, moe_w_down[0])
    xf = _combine(pos0, pos1, ys, xf, route, g2)
    return xf.reshape(BATCH, SEQ, D)
```

```python
import functools
import math

import jax
import jax.numpy as jnp
from jax import lax
from jax.experimental import pallas as pl
from jax.experimental.pallas import tpu as pltpu

F32 = jnp.float32
BF16 = jnp.bfloat16
F8 = jnp.float8_e4m3fn
F8_MAX = 448.0

D = 2048
BATCH = 4
SEQ = 4096
T = BATCH * SEQ
EPS = 1e-6
MASK_VALUE = -1e30

CHUNK = 128
GW = 2 * D
GROUPS = 8
GDIM = GW // GROUPS

HD = 64
NH = D // HD
NKV = 4
GQA = NH // NKV
BLK = 128
NB = SEQ // BLK
ROT = 16
THETA = 500000.0
QKV = (NH + 2 * NKV) * HD

DFF = 5632
NE = 8
DFE = 7168

LANES = 128
VMEM_LIMIT = 56 * 1024 * 1024

TM = 1024
TN = 512
TMG = 512
TMR = 512
TME = 512
TNE1 = 1024
TNE2 = 512
TMC = 256
TMD = 256
R = 2 * T + NE * TME
NT = R // TME


def _cp(sem):
    return pltpu.CompilerParams(dimension_semantics=sem, vmem_limit_bytes=VMEM_LIMIT)


def _ada_kernel(c_ref, w_ref, b_ref, o_ref):
    c = c_ref[...]
    sc = (c * jax.nn.sigmoid(c)).astype(BF16)
    o_ref[0] = jnp.dot(sc, w_ref[0].astype(BF16), preferred_element_type=F32) + b_ref[0]


def _ada_mod(c_pad, ada_w, ada_b):
    depth = ada_w.shape[0]
    n = ada_w.shape[2]
    tn = 1024
    return pl.pallas_call(
        _ada_kernel,
        grid=(depth, n // tn),
        in_specs=[
            pl.BlockSpec((8, D), lambda l, j: (0, 0)),
            pl.BlockSpec((1, D, tn), lambda l, j: (l, 0, j)),
            pl.BlockSpec((1, 1, tn), lambda l, j: (l, 0, j)),
        ],
        out_specs=pl.BlockSpec((1, 8, tn), lambda l, j: (l, 0, j)),
        out_shape=jax.ShapeDtypeStruct((depth, 8, n), F32),
        compiler_params=_cp(("arbitrary", "arbitrary")),
        name="ada_mod",
    )(c_pad, ada_w, ada_b.reshape(depth, 1, n))


def _norm_rows(x, g, a, b):
    ms = jnp.mean(x * x, axis=-1, keepdims=True)
    return (x * lax.rsqrt(ms + EPS) * g) * a + b


def _norm_prologue(x_ref, g_ref, sc_ref, sh_ref, h_ref, rows=256):
    g = g_ref[...]
    a = 1.0 + sc_ref[0]
    b = sh_ref[0]
    for r0 in range(0, x_ref.shape[0], rows):
        h_ref[r0:r0 + rows, :] = _norm_rows(x_ref[r0:r0 + rows, :], g, a, b).astype(BF16)


def _norm_specs(tm):
    return [
        pl.BlockSpec((tm, D), lambda i, j: (i, 0)),
        pl.BlockSpec((1, D), lambda i, j: (0, 0)),
        pl.BlockSpec((1, 1, D), lambda i, j: (i * tm // SEQ, 0, 0)),
        pl.BlockSpec((1, 1, D), lambda i, j: (i * tm // SEQ, 0, 0)),
    ]


def _gmlp_in_kernel(x_ref, g_ref, sc_ref, sh_ref, w_ref, b_ref, z_ref, st_ref,
                    h_scr, s1_scr, s2_scr, *, nj, jv0, nv):
    j = pl.program_id(1)

    @pl.when(j == 0)
    def _():
        _norm_prologue(x_ref, g_ref, sc_ref, sh_ref, h_scr)
        s1_scr[...] = jnp.zeros_like(s1_scr)
        s2_scr[...] = jnp.zeros_like(s2_scr)

    acc = jnp.dot(h_scr[...], w_ref[...], preferred_element_type=F32) + b_ref[...]
    z = 0.5 * acc * (1.0 + lax.erf(acc * (1.0 / math.sqrt(2.0))))
    z_ref[...] = z.astype(BF16)

    @pl.when(j >= jv0)
    def _():
        s1_scr[...] += jnp.sum(z, axis=-1, keepdims=True)
        s2_scr[...] += jnp.sum(z * z, axis=-1, keepdims=True)

    @pl.when(j == nj - 1)
    def _():
        mean = s1_scr[...] * (1.0 / nv)
        var = s2_scr[...] * (1.0 / nv) - mean * mean
        rstd = lax.rsqrt(var + EPS)
        lane = lax.broadcasted_iota(jnp.int32, st_ref.shape, 1)
        st_ref[...] = jnp.where(lane == 0, mean, jnp.where(lane == 1, rstd, 0.0))


def _gmlp_in(x, g, sc, sh, w, b):
    n = w.shape[1]
    nj = n // TN
    kern = functools.partial(_gmlp_in_kernel, nj=nj, jv0=GW // TN, nv=float(GW))
    return pl.pallas_call(
        kern,
        grid=(T // TM, nj),
        in_specs=_norm_specs(TM) + [
            pl.BlockSpec((D, TN), lambda i, j: (0, j)),
            pl.BlockSpec((1, TN), lambda i, j: (0, j)),
        ],
        out_specs=[
            pl.BlockSpec((TM, TN), lambda i, j: (i, j)),
            pl.BlockSpec((TM, LANES), lambda i, j: (i, 0)),
        ],
        out_shape=[
            jax.ShapeDtypeStruct((T, n), BF16),
            jax.ShapeDtypeStruct((T, LANES), F32),
        ],
        scratch_shapes=[
            pltpu.VMEM((TM, D), BF16),
            pltpu.VMEM((TM, 1), F32),
            pltpu.VMEM((TM, 1), F32),
        ],
        compiler_params=_cp(("arbitrary", "arbitrary")),
        name="gmlp_in",
    )(x, g, sc, sh, w, b.reshape(1, n))


def _gmlp_gate_kernel(u_ref, v_ref, st_ref, lng_ref, lnb_ref, ws_ref, bs_ref, o_ref):
    row = lax.broadcasted_iota(jnp.int32, (CHUNK, CHUNK), 0)
    col = lax.broadcasted_iota(jnp.int32, (CHUNK, CHUNK), 1)
    causal = row >= col
    ws = [jnp.where(causal, ws_ref[g], 0.0).astype(BF16) for g in range(GROUPS)]
    for c in range(TMG // CHUNK):
        rs = slice(c * CHUNK, (c + 1) * CHUNK)
        mean = st_ref[rs, 0:1]
        rstd = st_ref[rs, 1:2]
        for g in range(GROUPS):
            cs = slice(g * GDIM, (g + 1) * GDIM)
            vn = (v_ref[rs, cs].astype(F32) - mean) * rstd * lng_ref[:, cs] + lnb_ref[:, cs]
            mixed = jnp.dot(ws[g], vn.astype(BF16), preferred_element_type=F32) + bs_ref[g]
            o_ref[rs, cs] = (u_ref[rs, cs].astype(F32) * mixed).astype(BF16)


def _gmlp_gate(z, st, ln_g, ln_b, w_s, b_s):
    return pl.pallas_call(
        _gmlp_gate_kernel,
        grid=(T // TMG,),
        in_specs=[
            pl.BlockSpec((TMG, GW), lambda i: (i, 0)),
            pl.BlockSpec((TMG, GW), lambda i: (i, 1)),
            pl.BlockSpec((TMG, LANES), lambda i: (i, 0)),
            pl.BlockSpec((1, GW), lambda i: (0, 0)),
            pl.BlockSpec((1, GW), lambda i: (0, 0)),
            pl.BlockSpec((GROUPS, CHUNK, CHUNK), lambda i: (0, 0, 0)),
            pl.BlockSpec((GROUPS, CHUNK, 1), lambda i: (0, 0, 0)),
        ],
        out_specs=pl.BlockSpec((TMG, GW), lambda i: (i, 0)),
        out_shape=jax.ShapeDtypeStruct((T, GW), BF16),
        compiler_params=_cp(("arbitrary",)),
        name="gmlp_gate",
    )(z, z, st, ln_g.reshape(1, GW), ln_b.reshape(1, GW), w_s, b_s.reshape(GROUPS, CHUNK, 1))


def _res_gemm_kernel(a_ref, w_ref, x_ref, gate_ref, o_ref):
    y = jnp.dot(a_ref[...], w_ref[...], preferred_element_type=F32)
    o_ref[...] = x_ref[...] + gate_ref[0] * y


def _res_gemm(a, w, x, gate, *, tn, name):
    k = a.shape[1]
    return pl.pallas_call(
        _res_gemm_kernel,
        grid=(T // TM, D // tn),
        in_specs=[
            pl.BlockSpec((TM, k), lambda i, j: (i, 0)),
            pl.BlockSpec((k, tn), lambda i, j: (0, j)),
            pl.BlockSpec((TM, tn), lambda i, j: (i, j)),
            pl.BlockSpec((1, 1, tn), lambda i, j: (i * TM // SEQ, 0, j)),
        ],
        out_specs=pl.BlockSpec((TM, tn), lambda i, j: (i, j)),
        out_shape=jax.ShapeDtypeStruct((T, D), F32),
        compiler_params=_cp(("arbitrary", "arbitrary")),
        name=name,
    )(a, w, x, gate)


def _ffn_in_kernel(x_ref, g_ref, sc_ref, sh_ref, wg_ref, wu_ref, o_ref, h_scr):
    @pl.when(pl.program_id(1) == 0)
    def _():
        _norm_prologue(x_ref, g_ref, sc_ref, sh_ref, h_scr)

    h = h_scr[...]
    gate = jnp.dot(h, wg_ref[...], preferred_element_type=F32)
    up = jnp.dot(h, wu_ref[...], preferred_element_type=F32)
    o_ref[...] = (gate * jax.nn.sigmoid(gate) * up).astype(BF16)


def _ffn_in(x, g, sc, sh, w_gu):
    nj = DFF // TN
    return pl.pallas_call(
        _ffn_in_kernel,
        grid=(T // TM, nj),
        in_specs=_norm_specs(TM) + [
            pl.BlockSpec((D, TN), lambda i, j: (0, j)),
            pl.BlockSpec((D, TN), lambda i, j: (0, j + nj)),
        ],
        out_specs=pl.BlockSpec((TM, TN), lambda i, j: (i, j)),
        out_shape=jax.ShapeDtypeStruct((T, DFF), BF16),
        scratch_shapes=[pltpu.VMEM((TM, D), BF16)],
        compiler_params=_cp(("arbitrary", "arbitrary")),
        name="ffn_in",
    )(x, g, sc, sh, w_gu, w_gu)


def _qkv_kernel(x_ref, g_ref, sc_ref, sh_ref, pos_ref, invf_ref, w_ref, gain_ref, flag_ref,
                ind_ref, indt_ref, q_ref, k_ref, v_ref, h_scr, ct_scr, s1_scr, s2_scr):
    @pl.when(pl.program_id(1) == 0)
    def _():
        _norm_prologue(x_ref, g_ref, sc_ref, sh_ref, h_scr)
        lane = lax.broadcasted_iota(jnp.int32, ct_scr.shape, 1)
        m = lane & (HD - 1)
        ang = pos_ref[...] * invf_ref[...]
        c = jnp.cos(ang)
        s = jnp.sin(ang)
        ct_scr[...] = jnp.where(m < ROT, c, 1.0)
        s1_scr[...] = jnp.where(m < ROT // 2, -s, 0.0)
        s2_scr[...] = jnp.where((m >= ROT // 2) & (m < ROT), s, 0.0)

    y = jnp.dot(h_scr[...], w_ref[...], preferred_element_type=F32)
    y2 = y * y
    hi = y2.astype(BF16)
    lo = (y2 - hi.astype(F32)).astype(BF16)
    ind = ind_ref[...]
    ss = (jnp.dot(hi, ind, preferred_element_type=F32)
          + jnp.dot(lo, ind, preferred_element_type=F32))
    r = lax.rsqrt(ss * (1.0 / HD) + EPS)
    rhi = r.astype(BF16)
    rlo = (r - rhi.astype(F32)).astype(BF16)
    indt = indt_ref[...]
    rb = (jnp.dot(rhi, indt, preferred_element_type=F32)
          + jnp.dot(rlo, indt, preferred_element_type=F32))
    yn = y * rb * gain_ref[...]
    ct = ct_scr[...]
    s1 = s1_scr[...]
    s2 = s2_scr[...]
    parts = []
    for k in range(TN // LANES):
        yb = yn[:, k * LANES:(k + 1) * LANES]
        parts.append(yb * ct + pltpu.roll(yb, LANES - ROT // 2, 1) * s1
                     + pltpu.roll(yb, ROT // 2, 1) * s2)
    yr = jnp.concatenate(parts, axis=1)
    res = jnp.where(flag_ref[...] > 0.5, yr, y).astype(BF16)

    j = pl.program_id(1)

    @pl.when(j < NKV)
    def _():
        for blk in range(TM // BLK):
            rs = slice(blk * BLK, (blk + 1) * BLK)
            for gq in range(GQA):
                q_ref[blk, 0, gq * BLK:(gq + 1) * BLK, :] = res[rs, gq * HD:(gq + 1) * HD]

    @pl.when(j == NKV)
    def _():
        for blk in range(TM // BLK):
            rs = slice(blk * BLK, (blk + 1) * BLK)
            for kv in range(NKV):
                k_ref[blk, kv] = res[rs, kv * HD:(kv + 1) * HD]
                v_ref[blk, kv] = res[rs, (NKV + kv) * HD:(NKV + kv + 1) * HD]


def _qkv(x, g, sc, sh, pos, invf, w, gain, flag, ind, indt):
    nblk = TM // BLK
    return pl.pallas_call(
        _qkv_kernel,
        grid=(T // TM, QKV // TN),
        in_specs=_norm_specs(TM) + [
            pl.BlockSpec((TM, 1), lambda i, j: (i, 0)),
            pl.BlockSpec((1, LANES), lambda i, j: (0, 0)),
            pl.BlockSpec((D, TN), lambda i, j: (0, j)),
            pl.BlockSpec((1, TN), lambda i, j: (0, j)),
            pl.BlockSpec((1, TN), lambda i, j: (0, j)),
            pl.BlockSpec((TN, LANES), lambda i, j: (0, 0)),
            pl.BlockSpec((LANES, TN), lambda i, j: (0, 0)),
        ],
        out_specs=[
            pl.BlockSpec((nblk, 1, GQA * BLK, HD), lambda i, j: (i, jnp.minimum(j, NKV - 1), 0, 0)),
            pl.BlockSpec((nblk, NKV, BLK, HD), lambda i, j: (i, 0, 0, 0)),
            pl.BlockSpec((nblk, NKV, BLK, HD), lambda i, j: (i, 0, 0, 0)),
        ],
        out_shape=[
            jax.ShapeDtypeStruct((BATCH * NB, NKV, GQA * BLK, HD), BF16),
            jax.ShapeDtypeStruct((BATCH * NB, NKV, BLK, HD), BF16),
            jax.ShapeDtypeStruct((BATCH * NB, NKV, BLK, HD), BF16),
        ],
        scratch_shapes=[
            pltpu.VMEM((TM, D), BF16),
            pltpu.VMEM((TM, LANES), F32),
            pltpu.VMEM((TM, LANES), F32),
            pltpu.VMEM((TM, LANES), F32),
        ],
        compiler_params=_cp(("arbitrary", "arbitrary")),
        name="qkv",
    )(x, g, sc, sh, pos, invf, w, gain, flag, ind, indt)


def _attn_kernel(q_ref, kp_ref, kc_ref, vp_ref, vc_ref, sink_ref, o_ref):
    n = pl.program_id(1)
    rows = GQA * BLK
    qi = lax.broadcasted_iota(jnp.int32, (rows, 2 * BLK), 0) & (BLK - 1)
    ri = lax.broadcasted_iota(jnp.int32, (rows, 2 * BLK), 1)
    valid = (ri > qi) & (ri <= qi + BLK) & ((ri >= BLK) | (n > 0))
    for kv in range(NKV):
        q = q_ref[0, kv]
        kw = jnp.concatenate([kp_ref[0, kv], kc_ref[0, kv]], axis=0)
        vw = jnp.concatenate([vp_ref[0, kv], vc_ref[0, kv]], axis=0)
        s = lax.dot_general(q, kw, (((1,), (1,)), ((), ())), preferred_element_type=F32)
        s = jnp.where(valid, s, MASK_VALUE)
        sink = sink_ref[kv]
        m = jnp.maximum(jnp.max(s, axis=-1, keepdims=True), sink)
        p = jnp.exp(s - m)
        den = jnp.sum(p, axis=-1, keepdims=True) + jnp.exp(sink - m)
        probs = (p * (1.0 / den)).astype(BF16)
        o = jnp.dot(probs, vw, preferred_element_type=F32).astype(BF16)
        for gq in range(GQA):
            head = kv * GQA + gq
            o_ref[:, head * HD:(head + 1) * HD] = o[gq * BLK:(gq + 1) * BLK, :]


def _attention(q5, k5, v5, sink5):
    cur = lambda b, n: (b * NB + n, 0, 0, 0)
    prev = lambda b, n: (b * NB + jnp.maximum(n - 1, 0), 0, 0, 0)
    return pl.pallas_call(
        _attn_kernel,
        grid=(BATCH, NB),
        in_specs=[
            pl.BlockSpec((1, NKV, GQA * BLK, HD), cur),
            pl.BlockSpec((1, NKV, BLK, HD), prev),
            pl.BlockSpec((1, NKV, BLK, HD), cur),
            pl.BlockSpec((1, NKV, BLK, HD), prev),
            pl.BlockSpec((1, NKV, BLK, HD), cur),
            pl.BlockSpec((NKV, GQA * BLK, 1), lambda b, n: (0, 0, 0)),
        ],
        out_specs=pl.BlockSpec((BLK, NH * HD), lambda b, n: (b * NB + n, 0)),
        out_shape=jax.ShapeDtypeStruct((T, NH * HD), BF16),
        compiler_params=_cp(("arbitrary", "arbitrary")),
        name="attention",
    )(q5, k5, k5, v5, v5, sink5)


def _router_kernel(x_ref, g_ref, sc_ref, sh_ref, rw_ref, rb_ref, route_ref, cnt_ref, run_scr):
    i = pl.program_id(0)

    @pl.when(i == 0)
    def _():
        run_scr[...] = jnp.zeros_like(run_scr)

    h = _norm_rows(x_ref[...], g_ref[...], 1.0 + sc_ref[0], sh_ref[0])

    logits = jnp.dot(h, rw_ref[...], preferred_element_type=F32,
                     precision=lax.Precision.HIGHEST) + rb_ref[...]
    lane = lax.broadcasted_iota(jnp.int32, logits.shape, 1).astype(F32)
    neg = -jnp.inf
    lg = jnp.where(lane < NE, logits, neg)
    v0 = jnp.max(lg, axis=-1, keepdims=True)
    i0 = jnp.min(jnp.where(lg == v0, lane, float(LANES)), axis=-1, keepdims=True)
    lg2 = jnp.where(lane == i0, neg, lg)
    v1 = jnp.max(lg2, axis=-1, keepdims=True)
    i1 = jnp.min(jnp.where(lg2 == v1, lane, float(LANES)), axis=-1, keepdims=True)
    t = jnp.exp(v1 - v0)
    w0 = 1.0 / (1.0 + t)
    w1 = t / (1.0 + t)

    sel0 = lane == i0
    sel1 = lane == i1
    onehot = jnp.where(sel0 | sel1, 1.0, 0.0)
    tm = onehot.shape[0]
    rr = lax.broadcasted_iota(jnp.int32, (tm, tm), 0)
    cc = lax.broadcasted_iota(jnp.int32, (tm, tm), 1)
    lower = jnp.where(rr > cc, 1.0, 0.0).astype(BF16)
    before = jnp.dot(lower, onehot.astype(BF16), preferred_element_type=F32) + run_scr[...]
    rank0 = jnp.sum(jnp.where(sel0, before, 0.0), axis=-1, keepdims=True)
    rank1 = jnp.sum(jnp.where(sel1, before, 0.0), axis=-1, keepdims=True)
    run_scr[...] += jnp.sum(onehot, axis=0, keepdims=True)
    cnt_ref[...] = run_scr[...]

    route_ref[...] = jnp.where(
        lane == 0, i0, jnp.where(
            lane == 1, i1, jnp.where(
                lane == 2, rank0, jnp.where(
                    lane == 3, rank1, jnp.where(
                        lane == 4, w0, jnp.where(lane == 5, w1, 0.0))))))


def _router(x, g, sc, sh, rw_pad, rb_pad):
    return pl.pallas_call(
        _router_kernel,
        grid=(T // TMR,),
        in_specs=[
            pl.BlockSpec((TMR, D), lambda i: (i, 0)),
            pl.BlockSpec((1, D), lambda i: (0, 0)),
            pl.BlockSpec((1, 1, D), lambda i: (i * TMR // SEQ, 0, 0)),
            pl.BlockSpec((1, 1, D), lambda i: (i * TMR // SEQ, 0, 0)),
            pl.BlockSpec((D, LANES), lambda i: (0, 0)),
            pl.BlockSpec((1, LANES), lambda i: (0, 0)),
        ],
        out_specs=[
            pl.BlockSpec((TMR, LANES), lambda i: (i, 0)),
            pl.BlockSpec((1, LANES), lambda i: (0, 0)),
        ],
        out_shape=[
            jax.ShapeDtypeStruct((T, LANES), F32),
            jax.ShapeDtypeStruct((1, LANES), F32),
        ],
        scratch_shapes=[pltpu.VMEM((1, LANES), F32)],
        compiler_params=_cp(("arbitrary",)),
        name="router",
    )(x, g, sc, sh, rw_pad, rb_pad)


def _dispatch_kernel(p0_ref, p1_ref, x_ref, g_ref, sc_ref, sh_ref, xs_in_hbm, xs_hbm, hbuf, sem):
    del xs_in_hbm
    i = pl.program_id(0)
    n = pl.num_programs(0)
    slot = i % 2

    def drain(s):
        for _ in range(2):
            pltpu.make_async_copy(hbuf.at[s], xs_hbm.at[pl.ds(0, TMD)], sem.at[s]).wait()

    @pl.when(i >= 2)
    def _():
        drain(slot)

    hbuf[slot] = _norm_rows(x_ref[...], g_ref[...], 1.0 + sc_ref[0], sh_ref[0])

    def issue(t, carry):
        tok = i * TMD + t
        src = hbuf.at[slot, pl.ds(t, 1)]
        pltpu.make_async_copy(src, xs_hbm.at[pl.ds(p0_ref[tok], 1)], sem.at[slot]).start()
        pltpu.make_async_copy(src, xs_hbm.at[pl.ds(p1_ref[tok], 1)], sem.at[slot]).start()
        return carry
    lax.fori_loop(0, TMD, issue, 0, unroll=8)

    @pl.when(i == n - 1)
    def _():
        drain(1 - slot)
        drain(slot)


def _dispatch(pos0, pos1, x, g, sc, sh, xs_zero):
    return pl.pallas_call(
        _dispatch_kernel,
        grid_spec=pltpu.PrefetchScalarGridSpec(
            num_scalar_prefetch=2,
            grid=(T // TMD,),
            in_specs=[
                pl.BlockSpec((TMD, D), lambda i, p0, p1: (i, 0)),
                pl.BlockSpec((1, D), lambda i, p0, p1: (0, 0)),
                pl.BlockSpec((1, 1, D), lambda i, p0, p1: (i * TMD // SEQ, 0, 0)),
                pl.BlockSpec((1, 1, D), lambda i, p0, p1: (i * TMD // SEQ, 0, 0)),
                pl.BlockSpec(memory_space=pl.ANY),
            ],
            out_specs=pl.BlockSpec(memory_space=pl.ANY),
            scratch_shapes=[pltpu.VMEM((2, TMD, D), F32), pltpu.SemaphoreType.DMA((2,))],
        ),
        out_shape=jax.ShapeDtypeStruct((R, D), F32),
        input_output_aliases={6: 0},
        compiler_params=_cp(("arbitrary",)),
        name="moe_dispatch",
    )(pos0, pos1, x, g, sc, sh, xs_zero)


def _absmax(ref, rows):
    m = None
    for r0 in range(0, ref.shape[0], rows):
        c = jnp.max(jnp.abs(ref[r0:r0 + rows, :]), axis=0, keepdims=True)
        m = c if m is None else jnp.maximum(m, c)
    return jnp.max(m.astype(F32), axis=1, keepdims=True)


def _f8_scale(amax):
    return jnp.where(amax > 0.0, (0.5 * F8_MAX) / amax, 1.0)


def _to_f8(src_ref, dst_ref, scale, rows):
    for r0 in range(0, src_ref.shape[0], rows):
        dst_ref[r0:r0 + rows, :] = (src_ref[r0:r0 + rows, :].astype(F32) * scale).astype(F8)


def _splat(s):
    return jnp.broadcast_to(s, (8, LANES))


def _moe_up_kernel(te_ref, first_ref, nu_ref, xs_ref, wg_ref, wu_ref, o_ref,
                   wg_scr, wu_scr, x_scr, sw_scr, sx_scr):
    j = pl.program_id(0)
    i = pl.program_id(1)

    @pl.when(first_ref[i] == 1)
    def _():
        for k, (w_ref, w_scr) in enumerate(((wg_ref, wg_scr), (wu_ref, wu_scr))):
            s = _f8_scale(_absmax(w_ref.at[0], 256))
            _to_f8(w_ref.at[0], w_scr, s, 256)
            sw_scr[k] = _splat(s)

    @pl.when((i < nu_ref[0]) & (j == 0))
    def _():
        sx_scr[i] = _splat(_f8_scale(_absmax(xs_ref, 128)))

    @pl.when(i < nu_ref[0])
    def _():
        sx = sx_scr[i][0:1, 0:1]
        _to_f8(xs_ref, x_scr, sx, 128)
        x = x_scr[...]
        gate = jnp.dot(x, wg_scr[...], preferred_element_type=F32) * (1.0 / (sx * sw_scr[0][0:1, 0:1]))
        up = jnp.dot(x, wu_scr[...], preferred_element_type=F32) * (1.0 / (sx * sw_scr[1][0:1, 0:1]))
        o_ref[...] = (gate * jax.nn.sigmoid(gate) * up).astype(BF16)

    @pl.when(i >= nu_ref[0])
    def _():
        o_ref[...] = jnp.zeros_like(o_ref)


def _moe_up(te, first, nu, xs, w_gu):
    nj = DFE // TNE1
    row = lambda j, i, te, first, nu: (jnp.maximum(jnp.minimum(i, nu[0] - 1), 0), 0)
    return pl.pallas_call(
        _moe_up_kernel,
        grid_spec=pltpu.PrefetchScalarGridSpec(
            num_scalar_prefetch=3,
            grid=(nj, NT),
            in_specs=[
                pl.BlockSpec((TME, D), row),
                pl.BlockSpec((1, D, TNE1), lambda j, i, te, first, nu: (te[i], 0, j)),
                pl.BlockSpec((1, D, TNE1), lambda j, i, te, first, nu: (te[i], 0, j + nj)),
            ],
            out_specs=pl.BlockSpec((TME, TNE1), lambda j, i, te, first, nu: (i, j)),
            scratch_shapes=[
                pltpu.VMEM((D, TNE1), F8),
                pltpu.VMEM((D, TNE1), F8),
                pltpu.VMEM((TME, D), F8),
                pltpu.VMEM((2, 8, LANES), F32),
                pltpu.VMEM((NT, 8, LANES), F32),
            ],
        ),
        out_shape=jax.ShapeDtypeStruct((R, DFE), BF16),
        compiler_params=_cp(("arbitrary", "arbitrary")),
        name="moe_up",
    )(te, first, nu, xs, w_gu, w_gu)


def _moe_down_kernel(te_ref, first_ref, nu_ref, a_ref, w_ref, o_ref, w_scr, a_scr, sw_scr, sa_scr):
    j = pl.program_id(0)
    i = pl.program_id(1)

    @pl.when(first_ref[i] == 1)
    def _():
        s = _f8_scale(_absmax(w_ref.at[0], 512))
        _to_f8(w_ref.at[0], w_scr, s, 512)
        sw_scr[...] = _splat(s)

    @pl.when((i < nu_ref[0]) & (j == 0))
    def _():
        sa_scr[i] = _splat(_f8_scale(_absmax(a_ref, 128)))

    @pl.when(i < nu_ref[0])
    def _():
        sa = sa_scr[i][0:1, 0:1]
        _to_f8(a_ref, a_scr, sa, 128)
        y = jnp.dot(a_scr[...], w_scr[...], preferred_element_type=F32)
        o_ref[...] = y * (1.0 / (sa * sw_scr[0:1, 0:1]))

    @pl.when(i >= nu_ref[0])
    def _():
        o_ref[...] = jnp.zeros_like(o_ref)


def _moe_down(te, first, nu, a, w_d):
    row = lambda j, i, te, first, nu: (jnp.maximum(jnp.minimum(i, nu[0] - 1), 0), 0)
    return pl.pallas_call(
        _moe_down_kernel,
        grid_spec=pltpu.PrefetchScalarGridSpec(
            num_scalar_prefetch=3,
            grid=(D // TNE2, NT),
            in_specs=[
                pl.BlockSpec((TME, DFE), row),
                pl.BlockSpec((1, DFE, TNE2), lambda j, i, te, first, nu: (te[i], 0, j)),
            ],
            out_specs=pl.BlockSpec((TME, TNE2), lambda j, i, te, first, nu: (i, j)),
            scratch_shapes=[
                pltpu.VMEM((DFE, TNE2), F8),
                pltpu.VMEM((TME, DFE), F8),
                pltpu.VMEM((8, LANES), F32),
                pltpu.VMEM((NT, 8, LANES), F32),
            ],
        ),
        out_shape=jax.ShapeDtypeStruct((R, D), F32),
        compiler_params=pltpu.CompilerParams(
            dimension_semantics=("arbitrary", "arbitrary"),
            vmem_limit_bytes=60 * 1024 * 1024),
        name="moe_down",
    )(te, first, nu, a, w_d)


def _combine_kernel(p0_ref, p1_ref, ys_hbm, x_ref, route_ref, gate_ref, o_ref, buf, sem):
    i = pl.program_id(0)
    n = pl.num_programs(0)

    def row_copy(src, k, t, slot):
        return pltpu.make_async_copy(ys_hbm.at[pl.ds(src, 1)], buf.at[slot, k, pl.ds(t, 1)],
                                     sem.at[slot])

    def issue(step, slot):
        def body(t, carry):
            tok = step * TMC + t
            row_copy(p0_ref[tok], 0, t, slot).start()
            row_copy(p1_ref[tok], 1, t, slot).start()
            return carry
        lax.fori_loop(0, TMC, body, 0, unroll=8)

    def drain(slot):
        for k in range(2):
            pltpu.make_async_copy(ys_hbm.at[pl.ds(0, TMC)], buf.at[slot, k], sem.at[slot]).wait()

    @pl.when(i == 0)
    def _():
        issue(0, 0)

    @pl.when(i + 1 < n)
    def _():
        issue(i + 1, (i + 1) % 2)

    slot = i % 2
    drain(slot)
    w0 = route_ref[:, 4:5]
    w1 = route_ref[:, 5:6]
    y = w0 * buf[slot, 0] + w1 * buf[slot, 1]
    o_ref[...] = x_ref[...] + gate_ref[0] * y


def _combine(pos0, pos1, ys, x, route, gate):
    return pl.pallas_call(
        _combine_kernel,
        grid_spec=pltpu.PrefetchScalarGridSpec(
            num_scalar_prefetch=2,
            grid=(T // TMC,),
            in_specs=[
                pl.BlockSpec(memory_space=pl.ANY),
                pl.BlockSpec((TMC, D), lambda i, p0, p1: (i, 0)),
                pl.BlockSpec((TMC, LANES), lambda i, p0, p1: (i, 0)),
                pl.BlockSpec((1, 1, D), lambda i, p0, p1: (i * TMC // SEQ, 0, 0)),
            ],
            out_specs=pl.BlockSpec((TMC, D), lambda i, p0, p1: (i, 0)),
            scratch_shapes=[
                pltpu.VMEM((2, 2, TMC, D), F32),
                pltpu.SemaphoreType.DMA((2,)),
            ],
        ),
        out_shape=jax.ShapeDtypeStruct((T, D), F32),
        compiler_params=_cp(("arbitrary",)),
        name="moe_combine",
    )(pos0, pos1, ys, x, route, gate)


def _routing_plan(route, counts):
    e0 = route[:, 0].astype(jnp.int32)
    e1 = route[:, 1].astype(jnp.int32)
    r0 = route[:, 2].astype(jnp.int32)
    r1 = route[:, 3].astype(jnp.int32)
    cnt = counts[0, :NE].astype(jnp.int32)
    ntile = (cnt + TME - 1) // TME
    tile_end = jnp.cumsum(ntile)
    offs = (tile_end - ntile) * TME
    nu = tile_end[-1]
    tid = jnp.arange(NT, dtype=jnp.int32)
    te_raw = jnp.sum((tid[:, None] >= tile_end[None, :]).astype(jnp.int32), axis=1)
    te_last = jnp.sum((nu - 1 >= tile_end).astype(jnp.int32))
    te = jnp.where(tid < nu, te_raw, te_last).astype(jnp.int32)
    prev = jnp.concatenate([jnp.full((1,), -1, jnp.int32), te[:-1]])
    first = ((te != prev) & (tid < nu)).astype(jnp.int32)
    pos0 = offs[e0] + r0
    pos1 = offs[e1] + r1
    return pos0, pos1, te, first, nu.reshape(1).astype(jnp.int32)


def kernel(x, c, positions, ada_w, ada_b, norm_g, gmlp_w_in, gmlp_b_in, gmlp_ln_g, gmlp_ln_b,
           gmlp_w_s, gmlp_b_s, gmlp_w_out, attn_w_qkv, attn_q_norm_g, attn_k_norm_g, attn_sinks,
           attn_w_o, ffn_w_gate_up, ffn_w_down, moe_router_w, moe_router_b, moe_w_gate_up,
           moe_w_down):
    xf = x.reshape(T, D)
    c_pad = jnp.concatenate([c, jnp.zeros((8 - BATCH, D), F32)], axis=0)
    mod = _ada_mod(c_pad, ada_w, ada_b)[:, :BATCH, :]

    def mods(layer):
        return [mod[layer, :, k * D:(k + 1) * D].reshape(BATCH, 1, D) for k in range(6)]

    sh1, sc1, g1, sh2, sc2, g2 = mods(0)
    z, st = _gmlp_in(xf, norm_g[0, 0].reshape(1, D), sc1, sh1, gmlp_w_in[0].astype(BF16),
                     gmlp_b_in[0])
    gated = _gmlp_gate(z, st, gmlp_ln_g[0], gmlp_ln_b[0], gmlp_w_s[0], gmlp_b_s[0])
    xf = _res_gemm(gated, gmlp_w_out[0].astype(BF16), xf, g1, tn=TN, name="gmlp_out")
    act = _ffn_in(xf, norm_g[0, 1].reshape(1, D), sc2, sh2, ffn_w_gate_up[0].astype(BF16))
    xf = _res_gemm(act, ffn_w_down[0].astype(BF16), xf, g2, tn=TN, name="ffn_out")

    sh1, sc1, g1, sh2, sc2, g2 = mods(1)
    pos = positions.reshape(T, 1).astype(F32)
    inv_freq = THETA ** (-jnp.arange(0, ROT, 2, dtype=F32) / ROT)
    invf = jnp.tile(inv_freq, LANES // (ROT // 2)).reshape(1, LANES)
    scale = HD ** -0.5
    gain = jnp.concatenate([jnp.tile(attn_q_norm_g[0] * scale, NH),
                            jnp.tile(attn_k_norm_g[0], NKV),
                            jnp.ones((NKV * HD,), F32)]).reshape(1, QKV)
    flag = jnp.concatenate([jnp.ones(((NH + NKV) * HD,), F32),
                            jnp.zeros((NKV * HD,), F32)]).reshape(1, QKV)
    head_of_col = jnp.arange(TN, dtype=jnp.int32) // HD
    ind = (head_of_col[:, None] == jnp.arange(LANES, dtype=jnp.int32)[None, :]).astype(BF16)
    q5, k5, v5 = _qkv(xf, norm_g[1, 0].reshape(1, D), sc1, sh1, pos, invf,
                      attn_w_qkv[0].astype(BF16), gain, flag, ind, ind.T)
    sink5 = jnp.repeat(attn_sinks[0].astype(F32), BLK).reshape(NKV, GQA * BLK, 1)
    o = _attention(q5, k5, v5, sink5)
    xf = _res_gemm(o, attn_w_o[0].astype(BF16), xf, g1, tn=TN, name="attn_out")

    rw_pad = jnp.concatenate([moe_router_w[0].astype(F32),
                              jnp.zeros((D, LANES - NE), F32)], axis=1)
    rb_pad = jnp.concatenate([moe_router_b[0].astype(F32),
                              jnp.zeros((LANES - NE,), F32)]).reshape(1, LANES)
    route, counts = _router(xf, norm_g[1, 1].reshape(1, D), sc2, sh2, rw_pad, rb_pad)
    pos0, pos1, te, first, nu = _routing_plan(route, counts)
    xs = _dispatch(pos0, pos1, xf, norm_g[1, 1].reshape(1, D), sc2, sh2, jnp.zeros((R, D), F32))
    hmid = _moe_up(te, first, nu, xs, moe_w_gate_up[0])
    ys = _moe_down(te, first, nu, hmid, moe_w_down[0])
    xf = _combine(pos0, pos1, ys, xf, route, g2)
    return xf.reshape(BATCH, SEQ, D)
```

```python
import functools
import math

import jax
import jax.numpy as jnp
from jax import lax
from jax.experimental import pallas as pl
from jax.experimental.pallas import tpu as pltpu

F32 = jnp.float32
BF16 = jnp.bfloat16
F8 = jnp.float8_e4m3fn
F8_MAX = 448.0

D = 2048
BATCH = 4
SEQ = 4096
T = BATCH * SEQ
EPS = 1e-6
MASK_VALUE = -1e30

CHUNK = 128
GW = 2 * D
GROUPS = 8
GDIM = GW // GROUPS

HD = 64
NH = D // HD
NKV = 4
GQA = NH // NKV
BLK = 128
NB = SEQ // BLK
ROT = 16
THETA = 500000.0
QKV = (NH + 2 * NKV) * HD

DFF = 5632
NE = 8
DFE = 7168

LANES = 128
VMEM_LIMIT = 56 * 1024 * 1024

TM = 1024
TN = 512
TMG = 512
TMR = 512
TME = 512
TNE1 = 1024
TNE2 = 512
TMC = 256
TMD = 256
R = 2 * T + NE * TME
NT = R // TME


def _cp(sem):
    return pltpu.CompilerParams(dimension_semantics=sem, vmem_limit_bytes=VMEM_LIMIT)


def _ada_kernel(c_ref, w_ref, b_ref, o_ref):
    c = c_ref[...]
    sc = (c * jax.nn.sigmoid(c)).astype(BF16)
    o_ref[0] = jnp.dot(sc, w_ref[0].astype(BF16), preferred_element_type=F32) + b_ref[0]


def _ada_mod(c_pad, ada_w, ada_b):
    depth = ada_w.shape[0]
    n = ada_w.shape[2]
    tn = 1024
    return pl.pallas_call(
        _ada_kernel,
        grid=(depth, n // tn),
        in_specs=[
            pl.BlockSpec((8, D), lambda l, j: (0, 0)),
            pl.BlockSpec((1, D, tn), lambda l, j: (l, 0, j)),
            pl.BlockSpec((1, 1, tn), lambda l, j: (l, 0, j)),
        ],
        out_specs=pl.BlockSpec((1, 8, tn), lambda l, j: (l, 0, j)),
        out_shape=jax.ShapeDtypeStruct((depth, 8, n), F32),
        compiler_params=_cp(("arbitrary", "arbitrary")),
        name="ada_mod",
    )(c_pad, ada_w, ada_b.reshape(depth, 1, n))


def _norm_rows(x, g, a, b):
    ms = jnp.mean(x * x, axis=-1, keepdims=True)
    return (x * lax.rsqrt(ms + EPS) * g) * a + b


def _norm_prologue(x_ref, g_ref, sc_ref, sh_ref, h_ref, rows=256):
    g = g_ref[...]
    a = 1.0 + sc_ref[0]
    b = sh_ref[0]
    for r0 in range(0, x_ref.shape[0], rows):
        h_ref[r0:r0 + rows, :] = _norm_rows(x_ref[r0:r0 + rows, :], g, a, b).astype(BF16)


def _norm_specs(tm):
    return [
        pl.BlockSpec((tm, D), lambda i, j: (i, 0)),
        pl.BlockSpec((1, D), lambda i, j: (0, 0)),
        pl.BlockSpec((1, 1, D), lambda i, j: (i * tm // SEQ, 0, 0)),
        pl.BlockSpec((1, 1, D), lambda i, j: (i * tm // SEQ, 0, 0)),
    ]


def _gmlp_in_kernel(x_ref, g_ref, sc_ref, sh_ref, w_ref, b_ref, z_ref, st_ref,
                    h_scr, s1_scr, s2_scr, *, nj, jv0, nv):
    j = pl.program_id(1)

    @pl.when(j == 0)
    def _():
        _norm_prologue(x_ref, g_ref, sc_ref, sh_ref, h_scr)
        s1_scr[...] = jnp.zeros_like(s1_scr)
        s2_scr[...] = jnp.zeros_like(s2_scr)

    acc = jnp.dot(h_scr[...], w_ref[...], preferred_element_type=F32) + b_ref[...]
    z = 0.5 * acc * (1.0 + lax.erf(acc * (1.0 / math.sqrt(2.0))))
    z_ref[...] = z.astype(BF16)

    @pl.when(j >= jv0)
    def _():
        s1_scr[...] += jnp.sum(z, axis=-1, keepdims=True)
        s2_scr[...] += jnp.sum(z * z, axis=-1, keepdims=True)

    @pl.when(j == nj - 1)
    def _():
        mean = s1_scr[...] * (1.0 / nv)
        var = s2_scr[...] * (1.0 / nv) - mean * mean
        rstd = lax.rsqrt(var + EPS)
        lane = lax.broadcasted_iota(jnp.int32, st_ref.shape, 1)
        st_ref[...] = jnp.where(lane == 0, mean, jnp.where(lane == 1, rstd, 0.0))


def _gmlp_in(x, g, sc, sh, w, b):
    n = w.shape[1]
    nj = n // TN
    kern = functools.partial(_gmlp_in_kernel, nj=nj, jv0=GW // TN, nv=float(GW))
    return pl.pallas_call(
        kern,
        grid=(T // TM, nj),
        in_specs=_norm_specs(TM) + [
            pl.BlockSpec((D, TN), lambda i, j: (0, j)),
            pl.BlockSpec((1, TN), lambda i, j: (0, j)),
        ],
        out_specs=[
            pl.BlockSpec((TM, TN), lambda i, j: (i, j)),
            pl.BlockSpec((TM, LANES), lambda i, j: (i, 0)),
        ],
        out_shape=[
            jax.ShapeDtypeStruct((T, n), BF16),
            jax.ShapeDtypeStruct((T, LANES), F32),
        ],
        scratch_shapes=[
            pltpu.VMEM((TM, D), BF16),
            pltpu.VMEM((TM, 1), F32),
            pltpu.VMEM((TM, 1), F32),
        ],
        compiler_params=_cp(("arbitrary", "arbitrary")),
        name="gmlp_in",
    )(x, g, sc, sh, w, b.reshape(1, n))


def _gmlp_gate_kernel(u_ref, v_ref, st_ref, lng_ref, lnb_ref, ws_ref, bs_ref, o_ref):
    row = lax.broadcasted_iota(jnp.int32, (CHUNK, CHUNK), 0)
    col = lax.broadcasted_iota(jnp.int32, (CHUNK, CHUNK), 1)
    causal = row >= col
    ws = [jnp.where(causal, ws_ref[g], 0.0).astype(BF16) for g in range(GROUPS)]
    for c in range(TMG // CHUNK):
        rs = slice(c * CHUNK, (c + 1) * CHUNK)
        mean = st_ref[rs, 0:1]
        rstd = st_ref[rs, 1:2]
        for g in range(GROUPS):
            cs = slice(g * GDIM, (g + 1) * GDIM)
            vn = (v_ref[rs, cs].astype(F32) - mean) * rstd * lng_ref[:, cs] + lnb_ref[:, cs]
            mixed = jnp.dot(ws[g], vn.astype(BF16), preferred_element_type=F32) + bs_ref[g]
            o_ref[rs, cs] = (u_ref[rs, cs].astype(F32) * mixed).astype(BF16)


def _gmlp_gate(z, st, ln_g, ln_b, w_s, b_s):
    return pl.pallas_call(
        _gmlp_gate_kernel,
        grid=(T // TMG,),
        in_specs=[
            pl.BlockSpec((TMG, GW), lambda i: (i, 0)),
            pl.BlockSpec((TMG, GW), lambda i: (i, 1)),
            pl.BlockSpec((TMG, LANES), lambda i: (i, 0)),
            pl.BlockSpec((1, GW), lambda i: (0, 0)),
            pl.BlockSpec((1, GW), lambda i: (0, 0)),
            pl.BlockSpec((GROUPS, CHUNK, CHUNK), lambda i: (0, 0, 0)),
            pl.BlockSpec((GROUPS, CHUNK, 1), lambda i: (0, 0, 0)),
        ],
        out_specs=pl.BlockSpec((TMG, GW), lambda i: (i, 0)),
        out_shape=jax.ShapeDtypeStruct((T, GW), BF16),
        compiler_params=_cp(("arbitrary",)),
        name="gmlp_gate",
    )(z, z, st, ln_g.reshape(1, GW), ln_b.reshape(1, GW), w_s, b_s.reshape(GROUPS, CHUNK, 1))


def _res_gemm_kernel(a_ref, w_ref, x_ref, gate_ref, o_ref):
    y = jnp.dot(a_ref[...], w_ref[...], preferred_element_type=F32)
    o_ref[...] = x_ref[...] + gate_ref[0] * y


def _res_gemm(a, w, x, gate, *, tn, name):
    k = a.shape[1]
    return pl.pallas_call(
        _res_gemm_kernel,
        grid=(T // TM, D // tn),
        in_specs=[
            pl.BlockSpec((TM, k), lambda i, j: (i, 0)),
            pl.BlockSpec((k, tn), lambda i, j: (0, j)),
            pl.BlockSpec((TM, tn), lambda i, j: (i, j)),
            pl.BlockSpec((1, 1, tn), lambda i, j: (i * TM // SEQ, 0, j)),
        ],
        out_specs=pl.BlockSpec((TM, tn), lambda i, j: (i, j)),
        out_shape=jax.ShapeDtypeStruct((T, D), F32),
        compiler_params=_cp(("arbitrary", "arbitrary")),
        name=name,
    )(a, w, x, gate)


def _ffn_in_kernel(x_ref, g_ref, sc_ref, sh_ref, wg_ref, wu_ref, o_ref, h_scr):
    @pl.when(pl.program_id(1) == 0)
    def _():
        _norm_prologue(x_ref, g_ref, sc_ref, sh_ref, h_scr)

    h = h_scr[...]
    gate = jnp.dot(h, wg_ref[...], preferred_element_type=F32)
    up = jnp.dot(h, wu_ref[...], preferred_element_type=F32)
    o_ref[...] = (gate * jax.nn.sigmoid(gate) * up).astype(BF16)


def _ffn_in(x, g, sc, sh, w_gu):
    nj = DFF // TN
    return pl.pallas_call(
        _ffn_in_kernel,
        grid=(T // TM, nj),
        in_specs=_norm_specs(TM) + [
            pl.BlockSpec((D, TN), lambda i, j: (0, j)),
            pl.BlockSpec((D, TN), lambda i, j: (0, j + nj)),
        ],
        out_specs=pl.BlockSpec((TM, TN), lambda i, j: (i, j)),
        out_shape=jax.ShapeDtypeStruct((T, DFF), BF16),
        scratch_shapes=[pltpu.VMEM((TM, D), BF16)],
        compiler_params=_cp(("arbitrary", "arbitrary")),
        name="ffn_in",
    )(x, g, sc, sh, w_gu, w_gu)


def _qkv_kernel(x_ref, g_ref, sc_ref, sh_ref, pos_ref, invf_ref, w_ref, gain_ref, flag_ref,
                ind_ref, indt_ref, q_ref, k_ref, v_ref, h_scr, ct_scr, s1_scr, s2_scr):
    @pl.when(pl.program_id(1) == 0)
    def _():
        _norm_prologue(x_ref, g_ref, sc_ref, sh_ref, h_scr)
        lane = lax.broadcasted_iota(jnp.int32, ct_scr.shape, 1)
        m = lane & (HD - 1)
        ang = pos_ref[...] * invf_ref[...]
        c = jnp.cos(ang)
        s = jnp.sin(ang)
        ct_scr[...] = jnp.where(m < ROT, c, 1.0)
        s1_scr[...] = jnp.where(m < ROT // 2, -s, 0.0)
        s2_scr[...] = jnp.where((m >= ROT // 2) & (m < ROT), s, 0.0)

    y = jnp.dot(h_scr[...], w_ref[...], preferred_element_type=F32)
    y2 = y * y
    hi = y2.astype(BF16)
    lo = (y2 - hi.astype(F32)).astype(BF16)
    ind = ind_ref[...]
    ss = (jnp.dot(hi, ind, preferred_element_type=F32)
          + jnp.dot(lo, ind, preferred_element_type=F32))
    r = lax.rsqrt(ss * (1.0 / HD) + EPS)
    rhi = r.astype(BF16)
    rlo = (r - rhi.astype(F32)).astype(BF16)
    indt = indt_ref[...]
    rb = (jnp.dot(rhi, indt, preferred_element_type=F32)
          + jnp.dot(rlo, indt, preferred_element_type=F32))
    yn = y * rb * gain_ref[...]
    ct = ct_scr[...]
    s1 = s1_scr[...]
    s2 = s2_scr[...]
    parts = []
    for k in range(TN // LANES):
        yb = yn[:, k * LANES:(k + 1) * LANES]
        parts.append(yb * ct + pltpu.roll(yb, LANES - ROT // 2, 1) * s1
                     + pltpu.roll(yb, ROT // 2, 1) * s2)
    yr = jnp.concatenate(parts, axis=1)
    res = jnp.where(flag_ref[...] > 0.5, yr, y).astype(BF16)

    j = pl.program_id(1)

    @pl.when(j < NKV)
    def _():
        for blk in range(TM // BLK):
            rs = slice(blk * BLK, (blk + 1) * BLK)
            for gq in range(GQA):
                q_ref[blk, 0, gq * BLK:(gq + 1) * BLK, :] = res[rs, gq * HD:(gq + 1) * HD]

    @pl.when(j == NKV)
    def _():
        for blk in range(TM // BLK):
            rs = slice(blk * BLK, (blk + 1) * BLK)
            for kv in range(NKV):
                k_ref[blk, kv] = res[rs, kv * HD:(kv + 1) * HD]
                v_ref[blk, kv] = res[rs, (NKV + kv) * HD:(NKV + kv + 1) * HD]


def _qkv(x, g, sc, sh, pos, invf, w, gain, flag, ind, indt):
    nblk = TM // BLK
    return pl.pallas_call(
        _qkv_kernel,
        grid=(T // TM, QKV // TN),
        in_specs=_norm_specs(TM) + [
            pl.BlockSpec((TM, 1), lambda i, j: (i, 0)),
            pl.BlockSpec((1, LANES), lambda i, j: (0, 0)),
            pl.BlockSpec((D, TN), lambda i, j: (0, j)),
            pl.BlockSpec((1, TN), lambda i, j: (0, j)),
            pl.BlockSpec((1, TN), lambda i, j: (0, j)),
            pl.BlockSpec((TN, LANES), lambda i, j: (0, 0)),
            pl.BlockSpec((LANES, TN), lambda i, j: (0, 0)),
        ],
        out_specs=[
            pl.BlockSpec((nblk, 1, GQA * BLK, HD), lambda i, j: (i, jnp.minimum(j, NKV - 1), 0, 0)),
            pl.BlockSpec((nblk, NKV, BLK, HD), lambda i, j: (i, 0, 0, 0)),
            pl.BlockSpec((nblk, NKV, BLK, HD), lambda i, j: (i, 0, 0, 0)),
        ],
        out_shape=[
            jax.ShapeDtypeStruct((BATCH * NB, NKV, GQA * BLK, HD), BF16),
            jax.ShapeDtypeStruct((BATCH * NB, NKV, BLK, HD), BF16),
            jax.ShapeDtypeStruct((BATCH * NB, NKV, BLK, HD), BF16),
        ],
        scratch_shapes=[
            pltpu.VMEM((TM, D), BF16),
            pltpu.VMEM((TM, LANES), F32),
            pltpu.VMEM((TM, LANES), F32),
            pltpu.VMEM((TM, LANES), F32),
        ],
        compiler_params=_cp(("arbitrary", "arbitrary")),
        name="qkv",
    )(x, g, sc, sh, pos, invf, w, gain, flag, ind, indt)


def _attn_kernel(q_ref, kp_ref, kc_ref, vp_ref, vc_ref, sink_ref, o_ref, bias_scr):
    n = pl.program_id(1)
    ri = lax.broadcasted_iota(jnp.int32, (2 * BLK, BLK), 0)
    qi = lax.broadcasted_iota(jnp.int32, (2 * BLK, BLK), 1)
    valid = (ri > qi) & (ri <= qi + BLK) & ((ri >= BLK) | (n > 0))
    bias_scr[...] = jnp.where(valid, 0.0, MASK_VALUE)
    for kv in range(NKV):
        q = q_ref[0, kv]
        kw = jnp.concatenate([kp_ref[0, kv], kc_ref[0, kv]], axis=0)
        vw = jnp.concatenate([vp_ref[0, kv], vc_ref[0, kv]], axis=0)
        st = lax.dot_general(kw, q, (((1,), (1,)), ((), ())), preferred_element_type=F32)
        st = st + jnp.concatenate([bias_scr[...]] * GQA, axis=1)
        sink = sink_ref[kv]
        m = jnp.maximum(jnp.max(st, axis=0, keepdims=True), sink)
        p = jnp.exp(st - m)
        den = jnp.sum(p, axis=0, keepdims=True) + jnp.exp(sink - m)
        ot = lax.dot_general(vw, p.astype(BF16), (((0,), (0,)), ((), ())),
                             preferred_element_type=F32)
        ot = ot * (1.0 / den)
        for gq in range(GQA):
            head = kv * GQA + gq
            o_ref[:, head * HD:(head + 1) * HD] = ot[:, gq * BLK:(gq + 1) * BLK].T.astype(BF16)


def _attention(q5, k5, v5, sink5):
    cur = lambda b, n: (b * NB + n, 0, 0, 0)
    prev = lambda b, n: (b * NB + jnp.maximum(n - 1, 0), 0, 0, 0)
    return pl.pallas_call(
        _attn_kernel,
        grid=(BATCH, NB),
        in_specs=[
            pl.BlockSpec((1, NKV, GQA * BLK, HD), cur),
            pl.BlockSpec((1, NKV, BLK, HD), prev),
            pl.BlockSpec((1, NKV, BLK, HD), cur),
            pl.BlockSpec((1, NKV, BLK, HD), prev),
            pl.BlockSpec((1, NKV, BLK, HD), cur),
            pl.BlockSpec((NKV, 1, GQA * BLK), lambda b, n: (0, 0, 0)),
        ],
        out_specs=pl.BlockSpec((BLK, NH * HD), lambda b, n: (b * NB + n, 0)),
        out_shape=jax.ShapeDtypeStruct((T, NH * HD), BF16),
        scratch_shapes=[pltpu.VMEM((2 * BLK, BLK), F32)],
        compiler_params=_cp(("arbitrary", "arbitrary")),
        name="attention",
    )(q5, k5, k5, v5, v5, sink5)


def _router_kernel(x_ref, g_ref, sc_ref, sh_ref, rw_ref, rb_ref, route_ref, cnt_ref, run_scr):
    i = pl.program_id(0)

    @pl.when(i == 0)
    def _():
        run_scr[...] = jnp.zeros_like(run_scr)

    h = _norm_rows(x_ref[...], g_ref[...], 1.0 + sc_ref[0], sh_ref[0])

    logits = jnp.dot(h, rw_ref[...], preferred_element_type=F32,
                     precision=lax.Precision.HIGHEST) + rb_ref[...]
    lane = lax.broadcasted_iota(jnp.int32, logits.shape, 1).astype(F32)
    neg = -jnp.inf
    lg = jnp.where(lane < NE, logits, neg)
    v0 = jnp.max(lg, axis=-1, keepdims=True)
    i0 = jnp.min(jnp.where(lg == v0, lane, float(LANES)), axis=-1, keepdims=True)
    lg2 = jnp.where(lane == i0, neg, lg)
    v1 = jnp.max(lg2, axis=-1, keepdims=True)
    i1 = jnp.min(jnp.where(lg2 == v1, lane, float(LANES)), axis=-1, keepdims=True)
    t = jnp.exp(v1 - v0)
    w0 = 1.0 / (1.0 + t)
    w1 = t / (1.0 + t)

    sel0 = lane == i0
    sel1 = lane == i1
    onehot = jnp.where(sel0 | sel1, 1.0, 0.0)
    tm = onehot.shape[0]
    rr = lax.broadcasted_iota(jnp.int32, (tm, tm), 0)
    cc = lax.broadcasted_iota(jnp.int32, (tm, tm), 1)
    lower = jnp.where(rr > cc, 1.0, 0.0).astype(BF16)
    before = jnp.dot(lower, onehot.astype(BF16), preferred_element_type=F32) + run_scr[...]
    rank0 = jnp.sum(jnp.where(sel0, before, 0.0), axis=-1, keepdims=True)
    rank1 = jnp.sum(jnp.where(sel1, before, 0.0), axis=-1, keepdims=True)
    run_scr[...] += jnp.sum(onehot, axis=0, keepdims=True)
    cnt_ref[...] = run_scr[...]

    route_ref[...] = jnp.where(
        lane == 0, i0, jnp.where(
            lane == 1, i1, jnp.where(
                lane == 2, rank0, jnp.where(
                    lane == 3, rank1, jnp.where(
                        lane == 4, w0, jnp.where(lane == 5, w1, 0.0))))))


def _router(x, g, sc, sh, rw_pad, rb_pad):
    return pl.pallas_call(
        _router_kernel,
        grid=(T // TMR,),
        in_specs=[
            pl.BlockSpec((TMR, D), lambda i: (i, 0)),
            pl.BlockSpec((1, D), lambda i: (0, 0)),
            pl.BlockSpec((1, 1, D), lambda i: (i * TMR // SEQ, 0, 0)),
            pl.BlockSpec((1, 1, D), lambda i: (i * TMR // SEQ, 0, 0)),
            pl.BlockSpec((D, LANES), lambda i: (0, 0)),
            pl.BlockSpec((1, LANES), lambda i: (0, 0)),
        ],
        out_specs=[
            pl.BlockSpec((TMR, LANES), lambda i: (i, 0)),
            pl.BlockSpec((1, LANES), lambda i: (0, 0)),
        ],
        out_shape=[
            jax.ShapeDtypeStruct((T, LANES), F32),
            jax.ShapeDtypeStruct((1, LANES), F32),
        ],
        scratch_shapes=[pltpu.VMEM((1, LANES), F32)],
        compiler_params=_cp(("arbitrary",)),
        name="router",
    )(x, g, sc, sh, rw_pad, rb_pad)


def _dispatch_kernel(p0_ref, p1_ref, x_ref, g_ref, sc_ref, sh_ref, xs_in_hbm, xs_hbm, hbuf, sem):
    del xs_in_hbm
    i = pl.program_id(0)
    n = pl.num_programs(0)
    slot = i % 2

    def drain(s):
        for _ in range(2):
            pltpu.make_async_copy(hbuf.at[s], xs_hbm.at[pl.ds(0, TMD)], sem.at[s]).wait()

    @pl.when(i >= 2)
    def _():
        drain(slot)

    hbuf[slot] = _norm_rows(x_ref[...], g_ref[...], 1.0 + sc_ref[0], sh_ref[0])

    def issue(t, carry):
        tok = i * TMD + t
        src = hbuf.at[slot, pl.ds(t, 1)]
        pltpu.make_async_copy(src, xs_hbm.at[pl.ds(p0_ref[tok], 1)], sem.at[slot]).start()
        pltpu.make_async_copy(src, xs_hbm.at[pl.ds(p1_ref[tok], 1)], sem.at[slot]).start()
        return carry
    lax.fori_loop(0, TMD, issue, 0, unroll=8)

    @pl.when(i == n - 1)
    def _():
        drain(1 - slot)
        drain(slot)


def _dispatch(pos0, pos1, x, g, sc, sh, xs_zero):
    return pl.pallas_call(
        _dispatch_kernel,
        grid_spec=pltpu.PrefetchScalarGridSpec(
            num_scalar_prefetch=2,
            grid=(T // TMD,),
            in_specs=[
                pl.BlockSpec((TMD, D), lambda i, p0, p1: (i, 0)),
                pl.BlockSpec((1, D), lambda i, p0, p1: (0, 0)),
                pl.BlockSpec((1, 1, D), lambda i, p0, p1: (i * TMD // SEQ, 0, 0)),
                pl.BlockSpec((1, 1, D), lambda i, p0, p1: (i * TMD // SEQ, 0, 0)),
                pl.BlockSpec(memory_space=pl.ANY),
            ],
            out_specs=pl.BlockSpec(memory_space=pl.ANY),
            scratch_shapes=[pltpu.VMEM((2, TMD, D), F32), pltpu.SemaphoreType.DMA((2,))],
        ),
        out_shape=jax.ShapeDtypeStruct((R, D), F32),
        input_output_aliases={6: 0},
        compiler_params=_cp(("arbitrary",)),
        name="moe_dispatch",
    )(pos0, pos1, x, g, sc, sh, xs_zero)


def _absmax(ref, rows):
    m = None
    for r0 in range(0, ref.shape[0], rows):
        c = jnp.max(jnp.abs(ref[r0:r0 + rows, :]), axis=0, keepdims=True)
        m = c if m is None else jnp.maximum(m, c)
    return jnp.max(m.astype(F32), axis=1, keepdims=True)


def _f8_scale(amax):
    return jnp.where(amax > 0.0, (0.5 * F8_MAX) / amax, 1.0)


def _to_f8(src_ref, dst_ref, scale, rows):
    for r0 in range(0, src_ref.shape[0], rows):
        dst_ref[r0:r0 + rows, :] = (src_ref[r0:r0 + rows, :].astype(F32) * scale).astype(F8)


def _splat(s):
    return jnp.broadcast_to(s, (8, LANES))


def _moe_up_kernel(te_ref, first_ref, nu_ref, xs_ref, wg_ref, wu_ref, o_ref, so_ref,
                   wg_scr, wu_scr, x_scr, sw_scr, sx_scr):
    j = pl.program_id(0)
    i = pl.program_id(1)

    @pl.when(first_ref[i] == 1)
    def _():
        for k, (w_ref, w_scr) in enumerate(((wg_ref, wg_scr), (wu_ref, wu_scr))):
            s = _f8_scale(_absmax(w_ref.at[0], 256))
            _to_f8(w_ref.at[0], w_scr, s, 256)
            sw_scr[k] = _splat(s)

    @pl.when((i < nu_ref[0]) & (j == 0))
    def _():
        sx_scr[i] = _splat(_f8_scale(_absmax(xs_ref, 128)))

    @pl.when(i < nu_ref[0])
    def _():
        sx = sx_scr[i][0:1, 0:1]
        _to_f8(xs_ref, x_scr, sx, 128)
        x = x_scr[...]
        gate = jnp.dot(x, wg_scr[...], preferred_element_type=F32) * (1.0 / (sx * sw_scr[0][0:1, 0:1]))
        up = jnp.dot(x, wu_scr[...], preferred_element_type=F32) * (1.0 / (sx * sw_scr[1][0:1, 0:1]))
        act = gate * jax.nn.sigmoid(gate) * up
        amax = jnp.max(jnp.max(jnp.abs(act), axis=0, keepdims=True), axis=1, keepdims=True)
        so = _f8_scale(amax)
        o_ref[...] = (act * so).astype(F8)
        so_ref[0, 0] = _splat(so)

    @pl.when(i >= nu_ref[0])
    def _():
        o_ref[...] = jnp.zeros_like(o_ref)
        so_ref[0, 0] = jnp.ones((8, LANES), F32)


def _moe_up(te, first, nu, xs, w_gu):
    nj = DFE // TNE1
    row = lambda j, i, te, first, nu: (jnp.maximum(jnp.minimum(i, nu[0] - 1), 0), 0)
    return pl.pallas_call(
        _moe_up_kernel,
        grid_spec=pltpu.PrefetchScalarGridSpec(
            num_scalar_prefetch=3,
            grid=(nj, NT),
            in_specs=[
                pl.BlockSpec((TME, D), row),
                pl.BlockSpec((1, D, TNE1), lambda j, i, te, first, nu: (te[i], 0, j)),
                pl.BlockSpec((1, D, TNE1), lambda j, i, te, first, nu: (te[i], 0, j + nj)),
            ],
            out_specs=[
                pl.BlockSpec((TME, TNE1), lambda j, i, te, first, nu: (i, j)),
                pl.BlockSpec((1, 1, 8, LANES), lambda j, i, te, first, nu: (i, j, 0, 0)),
            ],
            scratch_shapes=[
                pltpu.VMEM((D, TNE1), F8),
                pltpu.VMEM((D, TNE1), F8),
                pltpu.VMEM((TME, D), F8),
                pltpu.VMEM((2, 8, LANES), F32),
                pltpu.VMEM((NT, 8, LANES), F32),
            ],
        ),
        out_shape=[
            jax.ShapeDtypeStruct((R, DFE), F8),
            jax.ShapeDtypeStruct((NT, nj, 8, LANES), F32),
        ],
        compiler_params=_cp(("arbitrary", "arbitrary")),
        name="moe_up",
    )(te, first, nu, xs, w_gu, w_gu)


def _moe_down_kernel(te_ref, first_ref, nu_ref, a_ref, sa_ref, w_ref, o_ref, w_scr, sw_scr):
    i = pl.program_id(1)

    @pl.when(first_ref[i] == 1)
    def _():
        s = _f8_scale(_absmax(w_ref.at[0], 512))
        _to_f8(w_ref.at[0], w_scr, s, 512)
        sw_scr[...] = _splat(s)

    @pl.when(i < nu_ref[0])
    def _():
        sw = sw_scr[0:1, 0:1]
        acc = None
        for c in range(DFE // TNE1):
            ks = slice(c * TNE1, (c + 1) * TNE1)
            part = jnp.dot(a_ref[:, ks], w_scr[ks, :], preferred_element_type=F32)
            part = part * (1.0 / (sa_ref[0, c][0:1, 0:1] * sw))
            acc = part if acc is None else acc + part
        o_ref[...] = acc

    @pl.when(i >= nu_ref[0])
    def _():
        o_ref[...] = jnp.zeros_like(o_ref)


def _moe_down(te, first, nu, a, sa, w_d):
    tile = lambda j, i, te, first, nu: jnp.maximum(jnp.minimum(i, nu[0] - 1), 0)
    return pl.pallas_call(
        _moe_down_kernel,
        grid_spec=pltpu.PrefetchScalarGridSpec(
            num_scalar_prefetch=3,
            grid=(D // TNE2, NT),
            in_specs=[
                pl.BlockSpec((TME, DFE), lambda *a: (tile(*a), 0)),
                pl.BlockSpec((1, DFE // TNE1, 8, LANES), lambda *a: (tile(*a), 0, 0, 0)),
                pl.BlockSpec((1, DFE, TNE2), lambda j, i, te, first, nu: (te[i], 0, j)),
            ],
            out_specs=pl.BlockSpec((TME, TNE2), lambda j, i, te, first, nu: (i, j)),
            scratch_shapes=[
                pltpu.VMEM((DFE, TNE2), F8),
                pltpu.VMEM((8, LANES), F32),
            ],
        ),
        out_shape=jax.ShapeDtypeStruct((R, D), F32),
        compiler_params=_cp(("arbitrary", "arbitrary")),
        name="moe_down",
    )(te, first, nu, a, sa, w_d)


def _combine_kernel(p0_ref, p1_ref, ys_hbm, x_ref, route_ref, gate_ref, o_ref, buf, sem):
    i = pl.program_id(0)
    n = pl.num_programs(0)

    def row_copy(src, k, t, slot):
        return pltpu.make_async_copy(ys_hbm.at[pl.ds(src, 1)], buf.at[slot, k, pl.ds(t, 1)],
                                     sem.at[slot])

    def issue(step, slot):
        def body(t, carry):
            tok = step * TMC + t
            row_copy(p0_ref[tok], 0, t, slot).start()
            row_copy(p1_ref[tok], 1, t, slot).start()
            return carry
        lax.fori_loop(0, TMC, body, 0, unroll=8)

    def drain(slot):
        for k in range(2):
            pltpu.make_async_copy(ys_hbm.at[pl.ds(0, TMC)], buf.at[slot, k], sem.at[slot]).wait()

    @pl.when(i == 0)
    def _():
        issue(0, 0)

    @pl.when(i + 1 < n)
    def _():
        issue(i + 1, (i + 1) % 2)

    slot = i % 2
    drain(slot)
    w0 = route_ref[:, 4:5]
    w1 = route_ref[:, 5:6]
    y = w0 * buf[slot, 0] + w1 * buf[slot, 1]
    o_ref[...] = x_ref[...] + gate_ref[0] * y


def _combine(pos0, pos1, ys, x, route, gate):
    return pl.pallas_call(
        _combine_kernel,
        grid_spec=pltpu.PrefetchScalarGridSpec(
            num_scalar_prefetch=2,
            grid=(T // TMC,),
            in_specs=[
                pl.BlockSpec(memory_space=pl.ANY),
                pl.BlockSpec((TMC, D), lambda i, p0, p1: (i, 0)),
                pl.BlockSpec((TMC, LANES), lambda i, p0, p1: (i, 0)),
                pl.BlockSpec((1, 1, D), lambda i, p0, p1: (i * TMC // SEQ, 0, 0)),
            ],
            out_specs=pl.BlockSpec((TMC, D), lambda i, p0, p1: (i, 0)),
            scratch_shapes=[
                pltpu.VMEM((2, 2, TMC, D), F32),
                pltpu.SemaphoreType.DMA((2,)),
            ],
        ),
        out_shape=jax.ShapeDtypeStruct((T, D), F32),
        compiler_params=_cp(("arbitrary",)),
        name="moe_combine",
    )(pos0, pos1, ys, x, route, gate)


def _routing_plan(route, counts):
    e0 = route[:, 0].astype(jnp.int32)
    e1 = route[:, 1].astype(jnp.int32)
    r0 = route[:, 2].astype(jnp.int32)
    r1 = route[:, 3].astype(jnp.int32)
    cnt = counts[0, :NE].astype(jnp.int32)
    ntile = (cnt + TME - 1) // TME
    tile_end = jnp.cumsum(ntile)
    offs = (tile_end - ntile) * TME
    nu = tile_end[-1]
    tid = jnp.arange(NT, dtype=jnp.int32)
    te_raw = jnp.sum((tid[:, None] >= tile_end[None, :]).astype(jnp.int32), axis=1)
    te_last = jnp.sum((nu - 1 >= tile_end).astype(jnp.int32))
    te = jnp.where(tid < nu, te_raw, te_last).astype(jnp.int32)
    prev = jnp.concatenate([jnp.full((1,), -1, jnp.int32), te[:-1]])
    first = ((te != prev) & (tid < nu)).astype(jnp.int32)
    pos0 = offs[e0] + r0
    pos1 = offs[e1] + r1
    return pos0, pos1, te, first, nu.reshape(1).astype(jnp.int32)


def kernel(x, c, positions, ada_w, ada_b, norm_g, gmlp_w_in, gmlp_b_in, gmlp_ln_g, gmlp_ln_b,
           gmlp_w_s, gmlp_b_s, gmlp_w_out, attn_w_qkv, attn_q_norm_g, attn_k_norm_g, attn_sinks,
           attn_w_o, ffn_w_gate_up, ffn_w_down, moe_router_w, moe_router_b, moe_w_gate_up,
           moe_w_down):
    xf = x.reshape(T, D)
    c_pad = jnp.concatenate([c, jnp.zeros((8 - BATCH, D), F32)], axis=0)
    mod = _ada_mod(c_pad, ada_w, ada_b)[:, :BATCH, :]

    def mods(layer):
        return [mod[layer, :, k * D:(k + 1) * D].reshape(BATCH, 1, D) for k in range(6)]

    sh1, sc1, g1, sh2, sc2, g2 = mods(0)
    z, st = _gmlp_in(xf, norm_g[0, 0].reshape(1, D), sc1, sh1, gmlp_w_in[0].astype(BF16),
                     gmlp_b_in[0])
    gated = _gmlp_gate(z, st, gmlp_ln_g[0], gmlp_ln_b[0], gmlp_w_s[0], gmlp_b_s[0])
    xf = _res_gemm(gated, gmlp_w_out[0].astype(BF16), xf, g1, tn=TN, name="gmlp_out")
    act = _ffn_in(xf, norm_g[0, 1].reshape(1, D), sc2, sh2, ffn_w_gate_up[0].astype(BF16))
    xf = _res_gemm(act, ffn_w_down[0].astype(BF16), xf, g2, tn=TN, name="ffn_out")

    sh1, sc1, g1, sh2, sc2, g2 = mods(1)
    pos = positions.reshape(T, 1).astype(F32)
    inv_freq = THETA ** (-jnp.arange(0, ROT, 2, dtype=F32) / ROT)
    invf = jnp.tile(inv_freq, LANES // (ROT // 2)).reshape(1, LANES)
    scale = HD ** -0.5
    gain = jnp.concatenate([jnp.tile(attn_q_norm_g[0] * scale, NH),
                            jnp.tile(attn_k_norm_g[0], NKV),
                            jnp.ones((NKV * HD,), F32)]).reshape(1, QKV)
    flag = jnp.concatenate([jnp.ones(((NH + NKV) * HD,), F32),
                            jnp.zeros((NKV * HD,), F32)]).reshape(1, QKV)
    head_of_col = jnp.arange(TN, dtype=jnp.int32) // HD
    ind = (head_of_col[:, None] == jnp.arange(LANES, dtype=jnp.int32)[None, :]).astype(BF16)
    q5, k5, v5 = _qkv(xf, norm_g[1, 0].reshape(1, D), sc1, sh1, pos, invf,
                      attn_w_qkv[0].astype(BF16), gain, flag, ind, ind.T)
    sink5 = jnp.repeat(attn_sinks[0].astype(F32), BLK).reshape(NKV, 1, GQA * BLK)
    o = _attention(q5, k5, v5, sink5)
    xf = _res_gemm(o, attn_w_o[0].astype(BF16), xf, g1, tn=TN, name="attn_out")

    rw_pad = jnp.concatenate([moe_router_w[0].astype(F32),
                              jnp.zeros((D, LANES - NE), F32)], axis=1)
    rb_pad = jnp.concatenate([moe_router_b[0].astype(F32),
                              jnp.zeros((LANES - NE,), F32)]).reshape(1, LANES)
    route, counts = _router(xf, norm_g[1, 1].reshape(1, D), sc2, sh2, rw_pad, rb_pad)
    pos0, pos1, te, first, nu = _routing_plan(route, counts)
    xs = _dispatch(pos0, pos1, xf, norm_g[1, 1].reshape(1, D), sc2, sh2, jnp.zeros((R, D), F32))
    hmid, hscale = _moe_up(te, first, nu, xs, moe_w_gate_up[0])
    ys = _moe_down(te, first, nu, hmid, hscale, moe_w_down[0])
    xf = _combine(pos0, pos1, ys, xf, route, g2)
    return xf.reshape(BATCH, SEQ, D)
```

```python
import functools
import math

import jax
import jax.numpy as jnp
from jax import lax
from jax.experimental import pallas as pl
from jax.experimental.pallas import tpu as pltpu

F32 = jnp.float32
BF16 = jnp.bfloat16
F8 = jnp.float8_e4m3fn
F8_MAX = 448.0

D = 2048
BATCH = 4
SEQ = 4096
T = BATCH * SEQ
EPS = 1e-6
MASK_VALUE = -1e30

CHUNK = 128
GW = 2 * D
GROUPS = 8
GDIM = GW // GROUPS

HD = 64
NH = D // HD
NKV = 4
GQA = NH // NKV
BLK = 128
NB = SEQ // BLK
ROT = 16
THETA = 500000.0
QKV = (NH + 2 * NKV) * HD

DFF = 5632
NE = 8
DFE = 7168

LANES = 128
VMEM_LIMIT = 56 * 1024 * 1024

TM = 1024
TN = 512
TNG = 1024
TMG = 512
TMR = 512
TME = 512
TNE1 = 1024
TNE2 = 512
TMC = 256
TMD = 256
R = 2 * T + NE * TME
NT = R // TME


def _cp(sem):
    return pltpu.CompilerParams(dimension_semantics=sem, vmem_limit_bytes=VMEM_LIMIT)


def _ada_kernel(c_ref, w_ref, b_ref, o_ref):
    c = c_ref[...]
    sc = (c * jax.nn.sigmoid(c)).astype(BF16)
    o_ref[0] = jnp.dot(sc, w_ref[0].astype(BF16), preferred_element_type=F32) + b_ref[0]


def _ada_mod(c_pad, ada_w, ada_b):
    depth = ada_w.shape[0]
    n = ada_w.shape[2]
    tn = 1024
    return pl.pallas_call(
        _ada_kernel,
        grid=(depth, n // tn),
        in_specs=[
            pl.BlockSpec((8, D), lambda l, j: (0, 0)),
            pl.BlockSpec((1, D, tn), lambda l, j: (l, 0, j)),
            pl.BlockSpec((1, 1, tn), lambda l, j: (l, 0, j)),
        ],
        out_specs=pl.BlockSpec((1, 8, tn), lambda l, j: (l, 0, j)),
        out_shape=jax.ShapeDtypeStruct((depth, 8, n), F32),
        compiler_params=_cp(("arbitrary", "arbitrary")),
        name="ada_mod",
    )(c_pad, ada_w, ada_b.reshape(depth, 1, n))


def _norm_rows(x, g, a, b):
    ms = jnp.mean(x * x, axis=-1, keepdims=True)
    return (x * lax.rsqrt(ms + EPS) * g) * a + b


def _norm_prologue(x_ref, g_ref, sc_ref, sh_ref, h_ref, rows=256):
    g = g_ref[...]
    a = 1.0 + sc_ref[0]
    b = sh_ref[0]
    for r0 in range(0, x_ref.shape[0], rows):
        h_ref[r0:r0 + rows, :] = _norm_rows(x_ref[r0:r0 + rows, :], g, a, b).astype(BF16)


def _norm_specs(tm):
    return [
        pl.BlockSpec((tm, D), lambda i, j: (i, 0)),
        pl.BlockSpec((1, D), lambda i, j: (0, 0)),
        pl.BlockSpec((1, 1, D), lambda i, j: (i * tm // SEQ, 0, 0)),
        pl.BlockSpec((1, 1, D), lambda i, j: (i * tm // SEQ, 0, 0)),
    ]


def _gmlp_in_kernel(x_ref, g_ref, sc_ref, sh_ref, w_ref, b_ref, z_ref, st_ref,
                    h_scr, s1_scr, s2_scr, *, nj, jv0, nv):
    j = pl.program_id(1)

    @pl.when(j == 0)
    def _():
        _norm_prologue(x_ref, g_ref, sc_ref, sh_ref, h_scr)
        s1_scr[...] = jnp.zeros_like(s1_scr)
        s2_scr[...] = jnp.zeros_like(s2_scr)

    acc = jnp.dot(h_scr[...], w_ref[...], preferred_element_type=F32) + b_ref[...]
    z = 0.5 * acc * (1.0 + lax.erf(acc * (1.0 / math.sqrt(2.0))))
    z_ref[...] = z.astype(BF16)

    @pl.when(j >= jv0)
    def _():
        s1_scr[...] += jnp.sum(z, axis=-1, keepdims=True)
        s2_scr[...] += jnp.sum(z * z, axis=-1, keepdims=True)

    @pl.when(j == nj - 1)
    def _():
        mean = s1_scr[...] * (1.0 / nv)
        var = s2_scr[...] * (1.0 / nv) - mean * mean
        rstd = lax.rsqrt(var + EPS)
        lane = lax.broadcasted_iota(jnp.int32, st_ref.shape, 1)
        st_ref[...] = jnp.where(lane == 0, mean, jnp.where(lane == 1, rstd, 0.0))


def _gmlp_in(x, g, sc, sh, w, b):
    n = w.shape[1]
    nj = n // TNG
    kern = functools.partial(_gmlp_in_kernel, nj=nj, jv0=GW // TNG, nv=float(GW))
    return pl.pallas_call(
        kern,
        grid=(T // TM, nj),
        in_specs=_norm_specs(TM) + [
            pl.BlockSpec((D, TNG), lambda i, j: (0, j)),
            pl.BlockSpec((1, TNG), lambda i, j: (0, j)),
        ],
        out_specs=[
            pl.BlockSpec((TM, TNG), lambda i, j: (i, j)),
            pl.BlockSpec((TM, LANES), lambda i, j: (i, 0)),
        ],
        out_shape=[
            jax.ShapeDtypeStruct((T, n), BF16),
            jax.ShapeDtypeStruct((T, LANES), F32),
        ],
        scratch_shapes=[
            pltpu.VMEM((TM, D), BF16),
            pltpu.VMEM((TM, 1), F32),
            pltpu.VMEM((TM, 1), F32),
        ],
        compiler_params=_cp(("arbitrary", "arbitrary")),
        name="gmlp_in",
    )(x, g, sc, sh, w, b.reshape(1, n))


def _gmlp_gate_kernel(u_ref, v_ref, st_ref, lng_ref, lnb_ref, ws_ref, bs_ref, o_ref):
    row = lax.broadcasted_iota(jnp.int32, (CHUNK, CHUNK), 0)
    col = lax.broadcasted_iota(jnp.int32, (CHUNK, CHUNK), 1)
    causal = row >= col
    ws = [jnp.where(causal, ws_ref[g], 0.0).astype(BF16) for g in range(GROUPS)]
    for c in range(TMG // CHUNK):
        rs = slice(c * CHUNK, (c + 1) * CHUNK)
        mean = st_ref[rs, 0:1]
        rstd = st_ref[rs, 1:2]
        for g in range(GROUPS):
            cs = slice(g * GDIM, (g + 1) * GDIM)
            vn = (v_ref[rs, cs].astype(F32) - mean) * rstd * lng_ref[:, cs] + lnb_ref[:, cs]
            mixed = jnp.dot(ws[g], vn.astype(BF16), preferred_element_type=F32) + bs_ref[g]
            o_ref[rs, cs] = (u_ref[rs, cs].astype(F32) * mixed).astype(BF16)


def _gmlp_gate(z, st, ln_g, ln_b, w_s, b_s):
    return pl.pallas_call(
        _gmlp_gate_kernel,
        grid=(T // TMG,),
        in_specs=[
            pl.BlockSpec((TMG, GW), lambda i: (i, 0)),
            pl.BlockSpec((TMG, GW), lambda i: (i, 1)),
            pl.BlockSpec((TMG, LANES), lambda i: (i, 0)),
            pl.BlockSpec((1, GW), lambda i: (0, 0)),
            pl.BlockSpec((1, GW), lambda i: (0, 0)),
            pl.BlockSpec((GROUPS, CHUNK, CHUNK), lambda i: (0, 0, 0)),
            pl.BlockSpec((GROUPS, CHUNK, 1), lambda i: (0, 0, 0)),
        ],
        out_specs=pl.BlockSpec((TMG, GW), lambda i: (i, 0)),
        out_shape=jax.ShapeDtypeStruct((T, GW), BF16),
        compiler_params=_cp(("arbitrary",)),
        name="gmlp_gate",
    )(z, z, st, ln_g.reshape(1, GW), ln_b.reshape(1, GW), w_s, b_s.reshape(GROUPS, CHUNK, 1))


def _res_gemm_kernel(a_ref, w_ref, x_ref, gate_ref, o_ref):
    y = jnp.dot(a_ref[...], w_ref[...], preferred_element_type=F32)
    o_ref[...] = x_ref[...] + gate_ref[0] * y


def _res_gemm(a, w, x, gate, *, tn, name):
    k = a.shape[1]
    return pl.pallas_call(
        _res_gemm_kernel,
        grid=(T // TM, D // tn),
        in_specs=[
            pl.BlockSpec((TM, k), lambda i, j: (i, 0)),
            pl.BlockSpec((k, tn), lambda i, j: (0, j)),
            pl.BlockSpec((TM, tn), lambda i, j: (i, j)),
            pl.BlockSpec((1, 1, tn), lambda i, j: (i * TM // SEQ, 0, j)),
        ],
        out_specs=pl.BlockSpec((TM, tn), lambda i, j: (i, j)),
        out_shape=jax.ShapeDtypeStruct((T, D), F32),
        compiler_params=_cp(("arbitrary", "arbitrary")),
        name=name,
    )(a, w, x, gate)


def _ffn_in_kernel(x_ref, g_ref, sc_ref, sh_ref, wg_ref, wu_ref, o_ref, h_scr):
    @pl.when(pl.program_id(1) == 0)
    def _():
        _norm_prologue(x_ref, g_ref, sc_ref, sh_ref, h_scr)

    h = h_scr[...]
    gate = jnp.dot(h, wg_ref[...], preferred_element_type=F32)
    up = jnp.dot(h, wu_ref[...], preferred_element_type=F32)
    o_ref[...] = (gate * jax.nn.sigmoid(gate) * up).astype(BF16)


def _ffn_in(x, g, sc, sh, w_gu):
    nj = DFF // TN
    return pl.pallas_call(
        _ffn_in_kernel,
        grid=(T // TM, nj),
        in_specs=_norm_specs(TM) + [
            pl.BlockSpec((D, TN), lambda i, j: (0, j)),
            pl.BlockSpec((D, TN), lambda i, j: (0, j + nj)),
        ],
        out_specs=pl.BlockSpec((TM, TN), lambda i, j: (i, j)),
        out_shape=jax.ShapeDtypeStruct((T, DFF), BF16),
        scratch_shapes=[pltpu.VMEM((TM, D), BF16)],
        compiler_params=_cp(("arbitrary", "arbitrary")),
        name="ffn_in",
    )(x, g, sc, sh, w_gu, w_gu)


def _qkv_kernel(x_ref, g_ref, sc_ref, sh_ref, pos_ref, invf_ref, w_ref, gain_ref, flag_ref,
                ind_ref, indt_ref, q_ref, k_ref, v_ref, h_scr, ct_scr, s1_scr, s2_scr):
    @pl.when(pl.program_id(1) == 0)
    def _():
        _norm_prologue(x_ref, g_ref, sc_ref, sh_ref, h_scr)
        lane = lax.broadcasted_iota(jnp.int32, ct_scr.shape, 1)
        m = lane & (HD - 1)
        ang = pos_ref[...] * invf_ref[...]
        c = jnp.cos(ang)
        s = jnp.sin(ang)
        ct_scr[...] = jnp.where(m < ROT, c, 1.0)
        s1_scr[...] = jnp.where(m < ROT // 2, -s, 0.0)
        s2_scr[...] = jnp.where((m >= ROT // 2) & (m < ROT), s, 0.0)

    y = jnp.dot(h_scr[...], w_ref[...], preferred_element_type=F32)
    y2 = y * y
    hi = y2.astype(BF16)
    lo = (y2 - hi.astype(F32)).astype(BF16)
    ind = ind_ref[...]
    ss = (jnp.dot(hi, ind, preferred_element_type=F32)
          + jnp.dot(lo, ind, preferred_element_type=F32))
    r = lax.rsqrt(ss * (1.0 / HD) + EPS)
    rhi = r.astype(BF16)
    rlo = (r - rhi.astype(F32)).astype(BF16)
    indt = indt_ref[...]
    rb = (jnp.dot(rhi, indt, preferred_element_type=F32)
          + jnp.dot(rlo, indt, preferred_element_type=F32))
    yn = y * rb * gain_ref[...]
    ct = ct_scr[...]
    s1 = s1_scr[...]
    s2 = s2_scr[...]
    parts = []
    for k in range(TN // LANES):
        yb = yn[:, k * LANES:(k + 1) * LANES]
        parts.append(yb * ct + pltpu.roll(yb, LANES - ROT // 2, 1) * s1
                     + pltpu.roll(yb, ROT // 2, 1) * s2)
    yr = jnp.concatenate(parts, axis=1)
    res = jnp.where(flag_ref[...] > 0.5, yr, y).astype(BF16)

    j = pl.program_id(1)

    @pl.when(j < NKV)
    def _():
        for blk in range(TM // BLK):
            rs = slice(blk * BLK, (blk + 1) * BLK)
            for gq in range(GQA):
                q_ref[blk, 0, gq * BLK:(gq + 1) * BLK, :] = res[rs, gq * HD:(gq + 1) * HD]

    @pl.when(j == NKV)
    def _():
        for blk in range(TM // BLK):
            rs = slice(blk * BLK, (blk + 1) * BLK)
            for kv in range(NKV):
                k_ref[blk, kv] = res[rs, kv * HD:(kv + 1) * HD]
                v_ref[blk, kv] = res[rs, (NKV + kv) * HD:(NKV + kv + 1) * HD]


def _qkv(x, g, sc, sh, pos, invf, w, gain, flag, ind, indt):
    nblk = TM // BLK
    return pl.pallas_call(
        _qkv_kernel,
        grid=(T // TM, QKV // TN),
        in_specs=_norm_specs(TM) + [
            pl.BlockSpec((TM, 1), lambda i, j: (i, 0)),
            pl.BlockSpec((1, LANES), lambda i, j: (0, 0)),
            pl.BlockSpec((D, TN), lambda i, j: (0, j)),
            pl.BlockSpec((1, TN), lambda i, j: (0, j)),
            pl.BlockSpec((1, TN), lambda i, j: (0, j)),
            pl.BlockSpec((TN, LANES), lambda i, j: (0, 0)),
            pl.BlockSpec((LANES, TN), lambda i, j: (0, 0)),
        ],
        out_specs=[
            pl.BlockSpec((nblk, 1, GQA * BLK, HD), lambda i, j: (i, jnp.minimum(j, NKV - 1), 0, 0)),
            pl.BlockSpec((nblk, NKV, BLK, HD), lambda i, j: (i, 0, 0, 0)),
            pl.BlockSpec((nblk, NKV, BLK, HD), lambda i, j: (i, 0, 0, 0)),
        ],
        out_shape=[
            jax.ShapeDtypeStruct((BATCH * NB, NKV, GQA * BLK, HD), BF16),
            jax.ShapeDtypeStruct((BATCH * NB, NKV, BLK, HD), BF16),
            jax.ShapeDtypeStruct((BATCH * NB, NKV, BLK, HD), BF16),
        ],
        scratch_shapes=[
            pltpu.VMEM((TM, D), BF16),
            pltpu.VMEM((TM, LANES), F32),
            pltpu.VMEM((TM, LANES), F32),
            pltpu.VMEM((TM, LANES), F32),
        ],
        compiler_params=_cp(("arbitrary", "arbitrary")),
        name="qkv",
    )(x, g, sc, sh, pos, invf, w, gain, flag, ind, indt)


def _attn_kernel(q_ref, kp_ref, kc_ref, vp_ref, vc_ref, sink_ref, o_ref, bias_scr):
    n = pl.program_id(1)
    ri = lax.broadcasted_iota(jnp.int32, (2 * BLK, BLK), 0)
    qi = lax.broadcasted_iota(jnp.int32, (2 * BLK, BLK), 1)
    valid = (ri > qi) & (ri <= qi + BLK) & ((ri >= BLK) | (n > 0))
    bias_scr[...] = jnp.where(valid, 0.0, MASK_VALUE)
    for kv in range(NKV):
        q = q_ref[0, kv]
        kw = jnp.concatenate([kp_ref[0, kv], kc_ref[0, kv]], axis=0)
        vw = jnp.concatenate([vp_ref[0, kv], vc_ref[0, kv]], axis=0)
        st = lax.dot_general(kw, q, (((1,), (1,)), ((), ())), preferred_element_type=F32)
        st = st + jnp.concatenate([bias_scr[...]] * GQA, axis=1)
        sink = sink_ref[kv]
        m = jnp.maximum(jnp.max(st, axis=0, keepdims=True), sink)
        p = jnp.exp(st - m)
        den = jnp.sum(p, axis=0, keepdims=True) + jnp.exp(sink - m)
        ot = lax.dot_general(vw, p.astype(BF16), (((0,), (0,)), ((), ())),
                             preferred_element_type=F32)
        ot = ot * (1.0 / den)
        for gq in range(GQA):
            head = kv * GQA + gq
            o_ref[:, head * HD:(head + 1) * HD] = ot[:, gq * BLK:(gq + 1) * BLK].T.astype(BF16)


def _attention(q5, k5, v5, sink5):
    cur = lambda b, n: (b * NB + n, 0, 0, 0)
    prev = lambda b, n: (b * NB + jnp.maximum(n - 1, 0), 0, 0, 0)
    return pl.pallas_call(
        _attn_kernel,
        grid=(BATCH, NB),
        in_specs=[
            pl.BlockSpec((1, NKV, GQA * BLK, HD), cur),
            pl.BlockSpec((1, NKV, BLK, HD), prev),
            pl.BlockSpec((1, NKV, BLK, HD), cur),
            pl.BlockSpec((1, NKV, BLK, HD), prev),
            pl.BlockSpec((1, NKV, BLK, HD), cur),
            pl.BlockSpec((NKV, 1, GQA * BLK), lambda b, n: (0, 0, 0)),
        ],
        out_specs=pl.BlockSpec((BLK, NH * HD), lambda b, n: (b * NB + n, 0)),
        out_shape=jax.ShapeDtypeStruct((T, NH * HD), BF16),
        scratch_shapes=[pltpu.VMEM((2 * BLK, BLK), F32)],
        compiler_params=_cp(("arbitrary", "arbitrary")),
        name="attention",
    )(q5, k5, k5, v5, v5, sink5)


def _router_kernel(x_ref, g_ref, sc_ref, sh_ref, rw_ref, rb_ref, route_ref, cnt_ref, run_scr):
    i = pl.program_id(0)

    @pl.when(i == 0)
    def _():
        run_scr[...] = jnp.zeros_like(run_scr)

    h = _norm_rows(x_ref[...], g_ref[...], 1.0 + sc_ref[0], sh_ref[0])

    rw = rw_ref[...]
    h_hi = h.astype(BF16)
    h_lo = (h - h_hi.astype(F32)).astype(BF16)
    w_hi = rw.astype(BF16)
    w_lo = (rw - w_hi.astype(F32)).astype(BF16)
    logits = (jnp.dot(h_hi, w_hi, preferred_element_type=F32)
              + jnp.dot(h_lo, w_hi, preferred_element_type=F32)
              + jnp.dot(h_hi, w_lo, preferred_element_type=F32)) + rb_ref[...]
    lane = lax.broadcasted_iota(jnp.int32, logits.shape, 1).astype(F32)
    neg = -jnp.inf
    lg = jnp.where(lane < NE, logits, neg)
    v0 = jnp.max(lg, axis=-1, keepdims=True)
    i0 = jnp.min(jnp.where(lg == v0, lane, float(LANES)), axis=-1, keepdims=True)
    lg2 = jnp.where(lane == i0, neg, lg)
    v1 = jnp.max(lg2, axis=-1, keepdims=True)
    i1 = jnp.min(jnp.where(lg2 == v1, lane, float(LANES)), axis=-1, keepdims=True)
    t = jnp.exp(v1 - v0)
    w0 = 1.0 / (1.0 + t)
    w1 = t / (1.0 + t)

    sel0 = lane == i0
    sel1 = lane == i1
    onehot = jnp.where(sel0 | sel1, 1.0, 0.0)
    tm = onehot.shape[0]
    rr = lax.broadcasted_iota(jnp.int32, (tm, tm), 0)
    cc = lax.broadcasted_iota(jnp.int32, (tm, tm), 1)
    lower = jnp.where(rr > cc, 1.0, 0.0).astype(BF16)
    before = jnp.dot(lower, onehot.astype(BF16), preferred_element_type=F32) + run_scr[...]
    rank0 = jnp.sum(jnp.where(sel0, before, 0.0), axis=-1, keepdims=True)
    rank1 = jnp.sum(jnp.where(sel1, before, 0.0), axis=-1, keepdims=True)
    run_scr[...] += jnp.sum(onehot, axis=0, keepdims=True)
    cnt_ref[...] = run_scr[...]

    route_ref[...] = jnp.where(
        lane == 0, i0, jnp.where(
            lane == 1, i1, jnp.where(
                lane == 2, rank0, jnp.where(
                    lane == 3, rank1, jnp.where(
                        lane == 4, w0, jnp.where(lane == 5, w1, 0.0))))))


def _router(x, g, sc, sh, rw_pad, rb_pad):
    return pl.pallas_call(
        _router_kernel,
        grid=(T // TMR,),
        in_specs=[
            pl.BlockSpec((TMR, D), lambda i: (i, 0)),
            pl.BlockSpec((1, D), lambda i: (0, 0)),
            pl.BlockSpec((1, 1, D), lambda i: (i * TMR // SEQ, 0, 0)),
            pl.BlockSpec((1, 1, D), lambda i: (i * TMR // SEQ, 0, 0)),
            pl.BlockSpec((D, LANES), lambda i: (0, 0)),
            pl.BlockSpec((1, LANES), lambda i: (0, 0)),
        ],
        out_specs=[
            pl.BlockSpec((TMR, LANES), lambda i: (i, 0)),
            pl.BlockSpec((1, LANES), lambda i: (0, 0)),
        ],
        out_shape=[
            jax.ShapeDtypeStruct((T, LANES), F32),
            jax.ShapeDtypeStruct((1, LANES), F32),
        ],
        scratch_shapes=[pltpu.VMEM((1, LANES), F32)],
        compiler_params=_cp(("arbitrary",)),
        name="router",
    )(x, g, sc, sh, rw_pad, rb_pad)


def _dispatch_kernel(p0_ref, p1_ref, x_ref, g_ref, sc_ref, sh_ref, xs_in_hbm, xs_hbm, hbuf, sem):
    del xs_in_hbm
    i = pl.program_id(0)
    n = pl.num_programs(0)
    slot = i % 2

    def drain(s):
        for _ in range(2):
            pltpu.make_async_copy(hbuf.at[s], xs_hbm.at[pl.ds(0, TMD)], sem.at[s]).wait()

    @pl.when(i >= 2)
    def _():
        drain(slot)

    hbuf[slot] = _norm_rows(x_ref[...], g_ref[...], 1.0 + sc_ref[0], sh_ref[0])

    def issue(t, carry):
        tok = i * TMD + t
        src = hbuf.at[slot, pl.ds(t, 1)]
        pltpu.make_async_copy(src, xs_hbm.at[pl.ds(p0_ref[tok], 1)], sem.at[slot]).start()
        pltpu.make_async_copy(src, xs_hbm.at[pl.ds(p1_ref[tok], 1)], sem.at[slot]).start()
        return carry
    lax.fori_loop(0, TMD, issue, 0, unroll=8)

    @pl.when(i == n - 1)
    def _():
        drain(1 - slot)
        drain(slot)


def _dispatch(pos0, pos1, x, g, sc, sh, xs_zero):
    return pl.pallas_call(
        _dispatch_kernel,
        grid_spec=pltpu.PrefetchScalarGridSpec(
            num_scalar_prefetch=2,
            grid=(T // TMD,),
            in_specs=[
                pl.BlockSpec((TMD, D), lambda i, p0, p1: (i, 0)),
                pl.BlockSpec((1, D), lambda i, p0, p1: (0, 0)),
                pl.BlockSpec((1, 1, D), lambda i, p0, p1: (i * TMD // SEQ, 0, 0)),
                pl.BlockSpec((1, 1, D), lambda i, p0, p1: (i * TMD // SEQ, 0, 0)),
                pl.BlockSpec(memory_space=pl.ANY),
            ],
            out_specs=pl.BlockSpec(memory_space=pl.ANY),
            scratch_shapes=[pltpu.VMEM((2, TMD, D), F32), pltpu.SemaphoreType.DMA((2,))],
        ),
        out_shape=jax.ShapeDtypeStruct((R, D), F32),
        input_output_aliases={6: 0},
        compiler_params=_cp(("arbitrary",)),
        name="moe_dispatch",
    )(pos0, pos1, x, g, sc, sh, xs_zero)


def _absmax(ref, rows):
    m = None
    for r0 in range(0, ref.shape[0], rows):
        c = jnp.max(jnp.abs(ref[r0:r0 + rows, :]), axis=0, keepdims=True)
        m = c if m is None else jnp.maximum(m, c)
    return jnp.max(m.astype(F32), axis=1, keepdims=True)


def _f8_scale(amax):
    return jnp.where(amax > 0.0, (0.5 * F8_MAX) / amax, 1.0)


def _to_f8(src_ref, dst_ref, scale, rows):
    for r0 in range(0, src_ref.shape[0], rows):
        dst_ref[r0:r0 + rows, :] = (src_ref[r0:r0 + rows, :].astype(F32) * scale).astype(F8)


def _splat(s):
    return jnp.broadcast_to(s, (8, LANES))


def _moe_up_kernel(te_ref, first_ref, nu_ref, xs_ref, wg_ref, wu_ref, o_ref, so_ref,
                   wg_scr, wu_scr, x_scr, sw_scr, sx_scr):
    j = pl.program_id(0)
    i = pl.program_id(1)

    @pl.when(first_ref[i] == 1)
    def _():
        for k, (w_ref, w_scr) in enumerate(((wg_ref, wg_scr), (wu_ref, wu_scr))):
            s = _f8_scale(_absmax(w_ref.at[0], 256))
            _to_f8(w_ref.at[0], w_scr, s, 256)
            sw_scr[k] = _splat(s)

    @pl.when((i < nu_ref[0]) & (j == 0))
    def _():
        sx_scr[i] = _splat(_f8_scale(_absmax(xs_ref, 128)))

    @pl.when(i < nu_ref[0])
    def _():
        sx = sx_scr[i][0:1, 0:1]
        _to_f8(xs_ref, x_scr, sx, 128)
        x = x_scr[...]
        gate = jnp.dot(x, wg_scr[...], preferred_element_type=F32) * (1.0 / (sx * sw_scr[0][0:1, 0:1]))
        up = jnp.dot(x, wu_scr[...], preferred_element_type=F32) * (1.0 / (sx * sw_scr[1][0:1, 0:1]))
        act = gate * jax.nn.sigmoid(gate) * up
        amax = jnp.max(jnp.max(jnp.abs(act), axis=0, keepdims=True), axis=1, keepdims=True)
        so = _f8_scale(amax)
        o_ref[...] = (act * so).astype(F8)
        so_ref[0, 0] = _splat(so)

    @pl.when(i >= nu_ref[0])
    def _():
        o_ref[...] = jnp.zeros_like(o_ref)
        so_ref[0, 0] = jnp.ones((8, LANES), F32)


def _moe_up(te, first, nu, xs, w_gu):
    nj = DFE // TNE1
    tile = lambda j, i, te, first, nu: jnp.maximum(jnp.minimum(i, nu[0] - 1), 0)
    return pl.pallas_call(
        _moe_up_kernel,
        grid_spec=pltpu.PrefetchScalarGridSpec(
            num_scalar_prefetch=3,
            grid=(nj, NT),
            in_specs=[
                pl.BlockSpec((TME, D), lambda *a: (tile(*a), 0)),
                pl.BlockSpec((1, D, TNE1), lambda j, i, te, first, nu: (te[i], 0, j)),
                pl.BlockSpec((1, D, TNE1), lambda j, i, te, first, nu: (te[i], 0, j + nj)),
            ],
            out_specs=[
                pl.BlockSpec((TME, TNE1), lambda j, i, te, first, nu: (i, j)),
                pl.BlockSpec((1, 1, 8, LANES), lambda j, i, te, first, nu: (i, j, 0, 0)),
            ],
            scratch_shapes=[
                pltpu.VMEM((D, TNE1), F8),
                pltpu.VMEM((D, TNE1), F8),
                pltpu.VMEM((TME, D), F8),
                pltpu.VMEM((2, 8, LANES), F32),
                pltpu.VMEM((NT, 8, LANES), F32),
            ],
        ),
        out_shape=[
            jax.ShapeDtypeStruct((R, DFE), F8),
            jax.ShapeDtypeStruct((NT, nj, 8, LANES), F32),
        ],
        compiler_params=_cp(("arbitrary", "arbitrary")),
        name="moe_up",
    )(te, first, nu, xs, w_gu, w_gu)


def _moe_down_kernel(te_ref, first_ref, nu_ref, a_ref, sa_ref, w_ref, o_ref, w_scr, sw_scr):
    i = pl.program_id(1)

    @pl.when(first_ref[i] == 1)
    def _():
        s = _f8_scale(_absmax(w_ref.at[0], 512))
        _to_f8(w_ref.at[0], w_scr, s, 512)
        sw_scr[...] = _splat(s)

    @pl.when(i < nu_ref[0])
    def _():
        sw = sw_scr[0:1, 0:1]
        acc = None
        for c in range(DFE // TNE1):
            ks = slice(c * TNE1, (c + 1) * TNE1)
            part = jnp.dot(a_ref[:, ks], w_scr[ks, :], preferred_element_type=F32)
            part = part * (1.0 / (sa_ref[0, c][0:1, 0:1] * sw))
            acc = part if acc is None else acc + part
        o_ref[...] = acc

    @pl.when(i >= nu_ref[0])
    def _():
        o_ref[...] = jnp.zeros_like(o_ref)


def _moe_down(te, first, nu, a, sa, w_d):
    tile = lambda j, i, te, first, nu: jnp.maximum(jnp.minimum(i, nu[0] - 1), 0)
    return pl.pallas_call(
        _moe_down_kernel,
        grid_spec=pltpu.PrefetchScalarGridSpec(
            num_scalar_prefetch=3,
            grid=(D // TNE2, NT),
            in_specs=[
                pl.BlockSpec((TME, DFE), lambda *a: (tile(*a), 0)),
                pl.BlockSpec((1, DFE // TNE1, 8, LANES), lambda *a: (tile(*a), 0, 0, 0)),
                pl.BlockSpec((1, DFE, TNE2), lambda j, i, te, first, nu: (te[i], 0, j)),
            ],
            out_specs=pl.BlockSpec((TME, TNE2), lambda j, i, te, first, nu: (i, j)),
            scratch_shapes=[
                pltpu.VMEM((DFE, TNE2), F8),
                pltpu.VMEM((8, LANES), F32),
            ],
        ),
        out_shape=jax.ShapeDtypeStruct((R, D), F32),
        compiler_params=_cp(("arbitrary", "arbitrary")),
        name="moe_down",
    )(te, first, nu, a, sa, w_d)


def _combine_kernel(p0_ref, p1_ref, ys_hbm, x_ref, route_ref, gate_ref, o_ref, buf, sem):
    i = pl.program_id(0)
    n = pl.num_programs(0)

    def row_copy(src, k, t, slot):
        return pltpu.make_async_copy(ys_hbm.at[pl.ds(src, 1)], buf.at[slot, k, pl.ds(t, 1)],
                                     sem.at[slot])

    def issue(step, slot):
        def body(t, carry):
            tok = step * TMC + t
            row_copy(p0_ref[tok], 0, t, slot).start()
            row_copy(p1_ref[tok], 1, t, slot).start()
            return carry
        lax.fori_loop(0, TMC, body, 0, unroll=8)

    def drain(slot):
        for k in range(2):
            pltpu.make_async_copy(ys_hbm.at[pl.ds(0, TMC)], buf.at[slot, k], sem.at[slot]).wait()

    @pl.when(i == 0)
    def _():
        issue(0, 0)

    @pl.when(i + 1 < n)
    def _():
        issue(i + 1, (i + 1) % 2)

    slot = i % 2
    drain(slot)
    w0 = route_ref[:, 4:5]
    w1 = route_ref[:, 5:6]
    y = w0 * buf[slot, 0] + w1 * buf[slot, 1]
    o_ref[...] = x_ref[...] + gate_ref[0] * y


def _combine(pos0, pos1, ys, x, route, gate):
    return pl.pallas_call(
        _combine_kernel,
        grid_spec=pltpu.PrefetchScalarGridSpec(
            num_scalar_prefetch=2,
            grid=(T // TMC,),
            in_specs=[
                pl.BlockSpec(memory_space=pl.ANY),
                pl.BlockSpec((TMC, D), lambda i, p0, p1: (i, 0)),
                pl.BlockSpec((TMC, LANES), lambda i, p0, p1: (i, 0)),
                pl.BlockSpec((1, 1, D), lambda i, p0, p1: (i * TMC // SEQ, 0, 0)),
            ],
            out_specs=pl.BlockSpec((TMC, D), lambda i, p0, p1: (i, 0)),
            scratch_shapes=[
                pltpu.VMEM((2, 2, TMC, D), F32),
                pltpu.SemaphoreType.DMA((2,)),
            ],
        ),
        out_shape=jax.ShapeDtypeStruct((T, D), F32),
        compiler_params=_cp(("arbitrary",)),
        name="moe_combine",
    )(pos0, pos1, ys, x, route, gate)


def _routing_plan(route, counts):
    e0 = route[:, 0].astype(jnp.int32)
    e1 = route[:, 1].astype(jnp.int32)
    r0 = route[:, 2].astype(jnp.int32)
    r1 = route[:, 3].astype(jnp.int32)
    cnt = counts[0, :NE].astype(jnp.int32)
    ntile = (cnt + TME - 1) // TME
    tile_end = jnp.cumsum(ntile)
    offs = (tile_end - ntile) * TME
    nu = tile_end[-1]
    tid = jnp.arange(NT, dtype=jnp.int32)
    te_raw = jnp.sum((tid[:, None] >= tile_end[None, :]).astype(jnp.int32), axis=1)
    te_last = jnp.sum((nu - 1 >= tile_end).astype(jnp.int32))
    te = jnp.where(tid < nu, te_raw, te_last).astype(jnp.int32)
    prev = jnp.concatenate([jnp.full((1,), -1, jnp.int32), te[:-1]])
    first = ((te != prev) & (tid < nu)).astype(jnp.int32)
    pos0 = offs[e0] + r0
    pos1 = offs[e1] + r1
    return pos0, pos1, te, first, nu.reshape(1).astype(jnp.int32)


def kernel(x, c, positions, ada_w, ada_b, norm_g, gmlp_w_in, gmlp_b_in, gmlp_ln_g, gmlp_ln_b,
           gmlp_w_s, gmlp_b_s, gmlp_w_out, attn_w_qkv, attn_q_norm_g, attn_k_norm_g, attn_sinks,
           attn_w_o, ffn_w_gate_up, ffn_w_down, moe_router_w, moe_router_b, moe_w_gate_up,
           moe_w_down):
    xf = x.reshape(T, D)
    c_pad = jnp.concatenate([c, jnp.zeros((8 - BATCH, D), F32)], axis=0)
    mod = _ada_mod(c_pad, ada_w, ada_b)[:, :BATCH, :]

    def mods(layer):
        return [mod[layer, :, k * D:(k + 1) * D].reshape(BATCH, 1, D) for k in range(6)]

    sh1, sc1, g1, sh2, sc2, g2 = mods(0)
    z, st = _gmlp_in(xf, norm_g[0, 0].reshape(1, D), sc1, sh1, gmlp_w_in[0].astype(BF16),
                     gmlp_b_in[0])
    gated = _gmlp_gate(z, st, gmlp_ln_g[0], gmlp_ln_b[0], gmlp_w_s[0], gmlp_b_s[0])
    xf = _res_gemm(gated, gmlp_w_out[0].astype(BF16), xf, g1, tn=2 * TN, name="gmlp_out")
    act = _ffn_in(xf, norm_g[0, 1].reshape(1, D), sc2, sh2, ffn_w_gate_up[0].astype(BF16))
    xf = _res_gemm(act, ffn_w_down[0].astype(BF16), xf, g2, tn=TN, name="ffn_out")

    sh1, sc1, g1, sh2, sc2, g2 = mods(1)
    pos = positions.reshape(T, 1).astype(F32)
    inv_freq = THETA ** (-jnp.arange(0, ROT, 2, dtype=F32) / ROT)
    invf = jnp.tile(inv_freq, LANES // (ROT // 2)).reshape(1, LANES)
    scale = HD ** -0.5
    gain = jnp.concatenate([jnp.tile(attn_q_norm_g[0] * scale, NH),
                            jnp.tile(attn_k_norm_g[0], NKV),
                            jnp.ones((NKV * HD,), F32)]).reshape(1, QKV)
    flag = jnp.concatenate([jnp.ones(((NH + NKV) * HD,), F32),
                            jnp.zeros((NKV * HD,), F32)]).reshape(1, QKV)
    head_of_col = jnp.arange(TN, dtype=jnp.int32) // HD
    ind = (head_of_col[:, None] == jnp.arange(LANES, dtype=jnp.int32)[None, :]).astype(BF16)
    q5, k5, v5 = _qkv(xf, norm_g[1, 0].reshape(1, D), sc1, sh1, pos, invf,
                      attn_w_qkv[0].astype(BF16), gain, flag, ind, ind.T)
    sink5 = jnp.repeat(attn_sinks[0].astype(F32), BLK).reshape(NKV, 1, GQA * BLK)
    o = _attention(q5, k5, v5, sink5)
    xf = _res_gemm(o, attn_w_o[0].astype(BF16), xf, g1, tn=2 * TN, name="attn_out")

    rw_pad = jnp.concatenate([moe_router_w[0].astype(F32),
                              jnp.zeros((D, LANES - NE), F32)], axis=1)
    rb_pad = jnp.concatenate([moe_router_b[0].astype(F32),
                              jnp.zeros((LANES - NE,), F32)]).reshape(1, LANES)
    route, counts = _router(xf, norm_g[1, 1].reshape(1, D), sc2, sh2, rw_pad, rb_pad)
    pos0, pos1, te, first, nu = _routing_plan(route, counts)
    xs = _dispatch(pos0, pos1, xf, norm_g[1, 1].reshape(1, D), sc2, sh2, jnp.zeros((R, D), F32))
    hmid, hscale = _moe_up(te, first, nu, xs, moe_w_gate_up[0])
    ys = _moe_down(te, first, nu, hmid, hscale, moe_w_down[0])
    xf = _combine(pos0, pos1, ys, xf, route, g2)
    return xf.reshape(BATCH, SEQ, D)
```

```python
import functools
import math

import jax
import jax.numpy as jnp
from jax import lax
from jax.experimental import pallas as pl
from jax.experimental.pallas import tpu as pltpu

F32 = jnp.float32
BF16 = jnp.bfloat16
F8 = jnp.float8_e4m3fn
F8_MAX = 448.0

D = 2048
BATCH = 4
SEQ = 4096
T = BATCH * SEQ
EPS = 1e-6
MASK_VALUE = -1e30

CHUNK = 128
GW = 2 * D
GROUPS = 8
GDIM = GW // GROUPS

HD = 64
NH = D // HD
NKV = 4
GQA = NH // NKV
BLK = 128
NB = SEQ // BLK
ROT = 16
THETA = 500000.0
QKV = (NH + 2 * NKV) * HD

DFF = 5632
NE = 8
DFE = 7168

LANES = 128
VMEM_LIMIT = 56 * 1024 * 1024

TM = 1024
TN = 512
TNG = 1024
TMG = 512
TMR = 512
TME = 512
TNE1 = 1024
TNE2 = 512
TMC = 256
TMD = 256
R = 2 * T + NE * TME
NT = R // TME


def _cp(sem):
    return pltpu.CompilerParams(dimension_semantics=sem, vmem_limit_bytes=VMEM_LIMIT)


def _ada_kernel(c_ref, w_ref, b_ref, o_ref):
    c = c_ref[...]
    sc = (c * jax.nn.sigmoid(c)).astype(BF16)
    o_ref[0] = jnp.dot(sc, w_ref[0].astype(BF16), preferred_element_type=F32) + b_ref[0]


def _ada_mod(c_pad, ada_w, ada_b):
    depth = ada_w.shape[0]
    n = ada_w.shape[2]
    tn = 1024
    return pl.pallas_call(
        _ada_kernel,
        grid=(depth, n // tn),
        in_specs=[
            pl.BlockSpec((8, D), lambda l, j: (0, 0)),
            pl.BlockSpec((1, D, tn), lambda l, j: (l, 0, j)),
            pl.BlockSpec((1, 1, tn), lambda l, j: (l, 0, j)),
        ],
        out_specs=pl.BlockSpec((1, 8, tn), lambda l, j: (l, 0, j)),
        out_shape=jax.ShapeDtypeStruct((depth, 8, n), F32),
        compiler_params=_cp(("arbitrary", "arbitrary")),
        name="ada_mod",
    )(c_pad, ada_w, ada_b.reshape(depth, 1, n))


def _norm_rows(x, g, a, b):
    ms = jnp.mean(x * x, axis=-1, keepdims=True)
    return (x * lax.rsqrt(ms + EPS) * g) * a + b


def _norm_prologue(x_ref, g_ref, sc_ref, sh_ref, h_ref, rows=256):
    g = g_ref[...]
    a = 1.0 + sc_ref[0]
    b = sh_ref[0]
    for r0 in range(0, x_ref.shape[0], rows):
        h_ref[r0:r0 + rows, :] = _norm_rows(x_ref[r0:r0 + rows, :], g, a, b).astype(BF16)


def _norm_specs(tm):
    return [
        pl.BlockSpec((tm, D), lambda i, j: (i, 0)),
        pl.BlockSpec((1, D), lambda i, j: (0, 0)),
        pl.BlockSpec((1, 1, D), lambda i, j: (i * tm // SEQ, 0, 0)),
        pl.BlockSpec((1, 1, D), lambda i, j: (i * tm // SEQ, 0, 0)),
    ]


def _gmlp_in_kernel(x_ref, g_ref, sc_ref, sh_ref, w_ref, b_ref, z_ref, st_ref,
                    h_scr, s1_scr, s2_scr, *, nj, jv0, nv):
    j = pl.program_id(1)

    @pl.when(j == 0)
    def _():
        _norm_prologue(x_ref, g_ref, sc_ref, sh_ref, h_scr)
        s1_scr[...] = jnp.zeros_like(s1_scr)
        s2_scr[...] = jnp.zeros_like(s2_scr)

    acc = jnp.dot(h_scr[...], w_ref[...], preferred_element_type=F32) + b_ref[...]
    z = 0.5 * acc * (1.0 + lax.erf(acc * (1.0 / math.sqrt(2.0))))
    z_ref[...] = z.astype(BF16)

    @pl.when(j >= jv0)
    def _():
        s1_scr[...] += jnp.sum(z, axis=-1, keepdims=True)
        s2_scr[...] += jnp.sum(z * z, axis=-1, keepdims=True)

    @pl.when(j == nj - 1)
    def _():
        mean = s1_scr[...] * (1.0 / nv)
        var = s2_scr[...] * (1.0 / nv) - mean * mean
        rstd = lax.rsqrt(var + EPS)
        lane = lax.broadcasted_iota(jnp.int32, st_ref.shape, 1)
        st_ref[...] = jnp.where(lane == 0, mean, jnp.where(lane == 1, rstd, 0.0))


def _gmlp_in(x, g, sc, sh, w, b):
    n = w.shape[1]
    nj = n // TNG
    kern = functools.partial(_gmlp_in_kernel, nj=nj, jv0=GW // TNG, nv=float(GW))
    return pl.pallas_call(
        kern,
        grid=(T // TM, nj),
        in_specs=_norm_specs(TM) + [
            pl.BlockSpec((D, TNG), lambda i, j: (0, j)),
            pl.BlockSpec((1, TNG), lambda i, j: (0, j)),
        ],
        out_specs=[
            pl.BlockSpec((TM, TNG), lambda i, j: (i, j)),
            pl.BlockSpec((TM, LANES), lambda i, j: (i, 0)),
        ],
        out_shape=[
            jax.ShapeDtypeStruct((T, n), BF16),
            jax.ShapeDtypeStruct((T, LANES), F32),
        ],
        scratch_shapes=[
            pltpu.VMEM((TM, D), BF16),
            pltpu.VMEM((TM, 1), F32),
            pltpu.VMEM((TM, 1), F32),
        ],
        compiler_params=_cp(("arbitrary", "arbitrary")),
        name="gmlp_in",
    )(x, g, sc, sh, w, b.reshape(1, n))


def _gmlp_gate_kernel(u_ref, v_ref, st_ref, lng_ref, lnb_ref, ws_ref, bs_ref, o_ref):
    row = lax.broadcasted_iota(jnp.int32, (CHUNK, CHUNK), 0)
    col = lax.broadcasted_iota(jnp.int32, (CHUNK, CHUNK), 1)
    causal = row >= col
    ws = [jnp.where(causal, ws_ref[g], 0.0).astype(BF16) for g in range(GROUPS)]
    for c in range(TMG // CHUNK):
        rs = slice(c * CHUNK, (c + 1) * CHUNK)
        mean = st_ref[rs, 0:1]
        rstd = st_ref[rs, 1:2]
        for g in range(GROUPS):
            cs = slice(g * GDIM, (g + 1) * GDIM)
            vn = (v_ref[rs, cs].astype(F32) - mean) * rstd * lng_ref[:, cs] + lnb_ref[:, cs]
            mixed = jnp.dot(ws[g], vn.astype(BF16), preferred_element_type=F32) + bs_ref[g]
            o_ref[rs, cs] = (u_ref[rs, cs].astype(F32) * mixed).astype(BF16)


def _gmlp_gate(z, st, ln_g, ln_b, w_s, b_s):
    return pl.pallas_call(
        _gmlp_gate_kernel,
        grid=(T // TMG,),
        in_specs=[
            pl.BlockSpec((TMG, GW), lambda i: (i, 0)),
            pl.BlockSpec((TMG, GW), lambda i: (i, 1)),
            pl.BlockSpec((TMG, LANES), lambda i: (i, 0)),
            pl.BlockSpec((1, GW), lambda i: (0, 0)),
            pl.BlockSpec((1, GW), lambda i: (0, 0)),
            pl.BlockSpec((GROUPS, CHUNK, CHUNK), lambda i: (0, 0, 0)),
            pl.BlockSpec((GROUPS, CHUNK, 1), lambda i: (0, 0, 0)),
        ],
        out_specs=pl.BlockSpec((TMG, GW), lambda i: (i, 0)),
        out_shape=jax.ShapeDtypeStruct((T, GW), BF16),
        compiler_params=_cp(("arbitrary",)),
        name="gmlp_gate",
    )(z, z, st, ln_g.reshape(1, GW), ln_b.reshape(1, GW), w_s, b_s.reshape(GROUPS, CHUNK, 1))


def _res_gemm_kernel(a_ref, w_ref, x_ref, gate_ref, o_ref):
    y = jnp.dot(a_ref[...], w_ref[...], preferred_element_type=F32)
    o_ref[...] = x_ref[...] + gate_ref[0] * y


def _res_gemm(a, w, x, gate, *, tn, name):
    k = a.shape[1]
    return pl.pallas_call(
        _res_gemm_kernel,
        grid=(T // TM, D // tn),
        in_specs=[
            pl.BlockSpec((TM, k), lambda i, j: (i, 0)),
            pl.BlockSpec((k, tn), lambda i, j: (0, j)),
            pl.BlockSpec((TM, tn), lambda i, j: (i, j)),
            pl.BlockSpec((1, 1, tn), lambda i, j: (i * TM // SEQ, 0, j)),
        ],
        out_specs=pl.BlockSpec((TM, tn), lambda i, j: (i, j)),
        out_shape=jax.ShapeDtypeStruct((T, D), F32),
        compiler_params=_cp(("arbitrary", "arbitrary")),
        name=name,
    )(a, w, x, gate)


def _ffn_in_kernel(x_ref, g_ref, sc_ref, sh_ref, wg_ref, wu_ref, o_ref, h_scr):
    @pl.when(pl.program_id(1) == 0)
    def _():
        _norm_prologue(x_ref, g_ref, sc_ref, sh_ref, h_scr)

    h = h_scr[...]
    gate = jnp.dot(h, wg_ref[...], preferred_element_type=F32)
    up = jnp.dot(h, wu_ref[...], preferred_element_type=F32)
    o_ref[...] = (gate * jax.nn.sigmoid(gate) * up).astype(BF16)


def _ffn_in(x, g, sc, sh, w_gu):
    nj = DFF // TN
    return pl.pallas_call(
        _ffn_in_kernel,
        grid=(T // TM, nj),
        in_specs=_norm_specs(TM) + [
            pl.BlockSpec((D, TN), lambda i, j: (0, j)),
            pl.BlockSpec((D, TN), lambda i, j: (0, j + nj)),
        ],
        out_specs=pl.BlockSpec((TM, TN), lambda i, j: (i, j)),
        out_shape=jax.ShapeDtypeStruct((T, DFF), BF16),
        scratch_shapes=[pltpu.VMEM((TM, D), BF16)],
        compiler_params=_cp(("arbitrary", "arbitrary")),
        name="ffn_in",
    )(x, g, sc, sh, w_gu, w_gu)


def _qkv_kernel(x_ref, g_ref, sc_ref, sh_ref, pos_ref, invf_ref, w_ref, gain_ref, flag_ref,
                ind_ref, indt_ref, q_ref, k_ref, v_ref, h_scr, ct_scr, s1_scr, s2_scr):
    @pl.when(pl.program_id(1) == 0)
    def _():
        _norm_prologue(x_ref, g_ref, sc_ref, sh_ref, h_scr)
        lane = lax.broadcasted_iota(jnp.int32, ct_scr.shape, 1)
        m = lane & (HD - 1)
        ang = pos_ref[...] * invf_ref[...]
        c = jnp.cos(ang)
        s = jnp.sin(ang)
        ct_scr[...] = jnp.where(m < ROT, c, 1.0)
        s1_scr[...] = jnp.where(m < ROT // 2, -s, 0.0)
        s2_scr[...] = jnp.where((m >= ROT // 2) & (m < ROT), s, 0.0)

    y = jnp.dot(h_scr[...], w_ref[...], preferred_element_type=F32)
    ss = jnp.dot((y * y).astype(BF16), ind_ref[...], preferred_element_type=F32)
    r = lax.rsqrt(ss * (1.0 / HD) + EPS)
    rhi = r.astype(BF16)
    rlo = (r - rhi.astype(F32)).astype(BF16)
    indt = indt_ref[...]
    rb = (jnp.dot(rhi, indt, preferred_element_type=F32)
          + jnp.dot(rlo, indt, preferred_element_type=F32))
    yn = y * rb * gain_ref[...]
    ct = ct_scr[...]
    s1 = s1_scr[...]
    s2 = s2_scr[...]
    parts = []
    for k in range(TN // LANES):
        yb = yn[:, k * LANES:(k + 1) * LANES]
        parts.append(yb * ct + pltpu.roll(yb, LANES - ROT // 2, 1) * s1
                     + pltpu.roll(yb, ROT // 2, 1) * s2)
    yr = jnp.concatenate(parts, axis=1)
    res = jnp.where(flag_ref[...] > 0.5, yr, y).astype(BF16)

    j = pl.program_id(1)

    @pl.when(j < NKV)
    def _():
        for blk in range(TM // BLK):
            rs = slice(blk * BLK, (blk + 1) * BLK)
            for gq in range(GQA):
                q_ref[blk, 0, gq * BLK:(gq + 1) * BLK, :] = res[rs, gq * HD:(gq + 1) * HD]

    @pl.when(j == NKV)
    def _():
        for blk in range(TM // BLK):
            rs = slice(blk * BLK, (blk + 1) * BLK)
            for kv in range(NKV):
                k_ref[blk, kv] = res[rs, kv * HD:(kv + 1) * HD]
                v_ref[blk, kv] = res[rs, (NKV + kv) * HD:(NKV + kv + 1) * HD]


def _qkv(x, g, sc, sh, pos, invf, w, gain, flag, ind, indt):
    nblk = TM // BLK
    return pl.pallas_call(
        _qkv_kernel,
        grid=(T // TM, QKV // TN),
        in_specs=_norm_specs(TM) + [
            pl.BlockSpec((TM, 1), lambda i, j: (i, 0)),
            pl.BlockSpec((1, LANES), lambda i, j: (0, 0)),
            pl.BlockSpec((D, TN), lambda i, j: (0, j)),
            pl.BlockSpec((1, TN), lambda i, j: (0, j)),
            pl.BlockSpec((1, TN), lambda i, j: (0, j)),
            pl.BlockSpec((TN, LANES), lambda i, j: (0, 0)),
            pl.BlockSpec((LANES, TN), lambda i, j: (0, 0)),
        ],
        out_specs=[
            pl.BlockSpec((nblk, 1, GQA * BLK, HD), lambda i, j: (i, jnp.minimum(j, NKV - 1), 0, 0)),
            pl.BlockSpec((nblk, NKV, BLK, HD), lambda i, j: (i, 0, 0, 0)),
            pl.BlockSpec((nblk, NKV, BLK, HD), lambda i, j: (i, 0, 0, 0)),
        ],
        out_shape=[
            jax.ShapeDtypeStruct((BATCH * NB, NKV, GQA * BLK, HD), BF16),
            jax.ShapeDtypeStruct((BATCH * NB, NKV, BLK, HD), BF16),
            jax.ShapeDtypeStruct((BATCH * NB, NKV, BLK, HD), BF16),
        ],
        scratch_shapes=[
            pltpu.VMEM((TM, D), BF16),
            pltpu.VMEM((TM, LANES), F32),
            pltpu.VMEM((TM, LANES), F32),
            pltpu.VMEM((TM, LANES), F32),
        ],
        compiler_params=_cp(("arbitrary", "arbitrary")),
        name="qkv",
    )(x, g, sc, sh, pos, invf, w, gain, flag, ind, indt)


def _attn_kernel(q_ref, kp_ref, kc_ref, vp_ref, vc_ref, sink_ref, o_ref, bias_scr):
    n = pl.program_id(1)
    ri = lax.broadcasted_iota(jnp.int32, (2 * BLK, BLK), 0)
    qi = lax.broadcasted_iota(jnp.int32, (2 * BLK, BLK), 1)
    valid = (ri > qi) & (ri <= qi + BLK) & ((ri >= BLK) | (n > 0))
    bias_scr[...] = jnp.where(valid, 0.0, MASK_VALUE)
    for kv in range(NKV):
        q = q_ref[0, kv]
        kw = jnp.concatenate([kp_ref[0, kv], kc_ref[0, kv]], axis=0)
        vw = jnp.concatenate([vp_ref[0, kv], vc_ref[0, kv]], axis=0)
        st = lax.dot_general(kw, q, (((1,), (1,)), ((), ())), preferred_element_type=F32)
        st = st + jnp.concatenate([bias_scr[...]] * GQA, axis=1)
        sink = sink_ref[kv]
        m = jnp.maximum(jnp.max(st, axis=0, keepdims=True), sink)
        p = jnp.exp(st - m)
        den = jnp.sum(p, axis=0, keepdims=True) + jnp.exp(sink - m)
        ot = lax.dot_general(vw, p.astype(BF16), (((0,), (0,)), ((), ())),
                             preferred_element_type=F32)
        ot = ot * (1.0 / den)
        for gq in range(GQA):
            head = kv * GQA + gq
            o_ref[:, head * HD:(head + 1) * HD] = ot[:, gq * BLK:(gq + 1) * BLK].T.astype(BF16)


def _attention(q5, k5, v5, sink5):
    cur = lambda b, n: (b * NB + n, 0, 0, 0)
    prev = lambda b, n: (b * NB + jnp.maximum(n - 1, 0), 0, 0, 0)
    return pl.pallas_call(
        _attn_kernel,
        grid=(BATCH, NB),
        in_specs=[
            pl.BlockSpec((1, NKV, GQA * BLK, HD), cur),
            pl.BlockSpec((1, NKV, BLK, HD), prev),
            pl.BlockSpec((1, NKV, BLK, HD), cur),
            pl.BlockSpec((1, NKV, BLK, HD), prev),
            pl.BlockSpec((1, NKV, BLK, HD), cur),
            pl.BlockSpec((NKV, 1, GQA * BLK), lambda b, n: (0, 0, 0)),
        ],
        out_specs=pl.BlockSpec((BLK, NH * HD), lambda b, n: (b * NB + n, 0)),
        out_shape=jax.ShapeDtypeStruct((T, NH * HD), BF16),
        scratch_shapes=[pltpu.VMEM((2 * BLK, BLK), F32)],
        compiler_params=_cp(("arbitrary", "arbitrary")),
        name="attention",
    )(q5, k5, k5, v5, v5, sink5)


def _router_kernel(x_ref, g_ref, sc_ref, sh_ref, rw_ref, rb_ref, route_ref, cnt_ref, run_scr):
    i = pl.program_id(0)

    @pl.when(i == 0)
    def _():
        run_scr[...] = jnp.zeros_like(run_scr)

    h = _norm_rows(x_ref[...], g_ref[...], 1.0 + sc_ref[0], sh_ref[0])

    rw = rw_ref[...]
    h_hi = h.astype(BF16)
    h_lo = (h - h_hi.astype(F32)).astype(BF16)
    w_hi = rw.astype(BF16)
    w_lo = (rw - w_hi.astype(F32)).astype(BF16)
    logits = (jnp.dot(h_hi, w_hi, preferred_element_type=F32)
              + jnp.dot(h_lo, w_hi, preferred_element_type=F32)
              + jnp.dot(h_hi, w_lo, preferred_element_type=F32)) + rb_ref[...]
    lane = lax.broadcasted_iota(jnp.int32, logits.shape, 1).astype(F32)
    neg = -jnp.inf
    lg = jnp.where(lane < NE, logits, neg)
    v0 = jnp.max(lg, axis=-1, keepdims=True)
    i0 = jnp.min(jnp.where(lg == v0, lane, float(LANES)), axis=-1, keepdims=True)
    lg2 = jnp.where(lane == i0, neg, lg)
    v1 = jnp.max(lg2, axis=-1, keepdims=True)
    i1 = jnp.min(jnp.where(lg2 == v1, lane, float(LANES)), axis=-1, keepdims=True)
    t = jnp.exp(v1 - v0)
    w0 = 1.0 / (1.0 + t)
    w1 = t / (1.0 + t)

    sel0 = lane == i0
    sel1 = lane == i1
    onehot = jnp.where(sel0 | sel1, 1.0, 0.0)
    tm = onehot.shape[0]
    rr = lax.broadcasted_iota(jnp.int32, (tm, tm), 0)
    cc = lax.broadcasted_iota(jnp.int32, (tm, tm), 1)
    lower = jnp.where(rr > cc, 1.0, 0.0).astype(BF16)
    before = jnp.dot(lower, onehot.astype(BF16), preferred_element_type=F32) + run_scr[...]
    rank0 = jnp.sum(jnp.where(sel0, before, 0.0), axis=-1, keepdims=True)
    rank1 = jnp.sum(jnp.where(sel1, before, 0.0), axis=-1, keepdims=True)
    run_scr[...] += jnp.sum(onehot, axis=0, keepdims=True)
    cnt_ref[...] = run_scr[...]

    route_ref[...] = jnp.where(
        lane == 0, i0, jnp.where(
            lane == 1, i1, jnp.where(
                lane == 2, rank0, jnp.where(
                    lane == 3, rank1, jnp.where(
                        lane == 4, w0, jnp.where(lane == 5, w1, 0.0))))))


def _router(x, g, sc, sh, rw_pad, rb_pad):
    return pl.pallas_call(
        _router_kernel,
        grid=(T // TMR,),
        in_specs=[
            pl.BlockSpec((TMR, D), lambda i: (i, 0)),
            pl.BlockSpec((1, D), lambda i: (0, 0)),
            pl.BlockSpec((1, 1, D), lambda i: (i * TMR // SEQ, 0, 0)),
            pl.BlockSpec((1, 1, D), lambda i: (i * TMR // SEQ, 0, 0)),
            pl.BlockSpec((D, LANES), lambda i: (0, 0)),
            pl.BlockSpec((1, LANES), lambda i: (0, 0)),
        ],
        out_specs=[
            pl.BlockSpec((TMR, LANES), lambda i: (i, 0)),
            pl.BlockSpec((1, LANES), lambda i: (0, 0)),
        ],
        out_shape=[
            jax.ShapeDtypeStruct((T, LANES), F32),
            jax.ShapeDtypeStruct((1, LANES), F32),
        ],
        scratch_shapes=[pltpu.VMEM((1, LANES), F32)],
        compiler_params=_cp(("arbitrary",)),
        name="router",
    )(x, g, sc, sh, rw_pad, rb_pad)


def _dispatch_kernel(p0_ref, p1_ref, zs_ref, nu_ref, x_ref, g_ref, sc_ref, sh_ref, xs_hbm,
                     hbuf, sem):
    i = pl.program_id(0)
    n = pl.num_programs(0)
    slot = i % 2

    def drain(s):
        for _ in range(2):
            pltpu.make_async_copy(hbuf.at[s], xs_hbm.at[pl.ds(0, TMD)], sem.at[s]).wait()

    def zero_tile(row0):
        copies = [pltpu.make_async_copy(hbuf.at[half], xs_hbm.at[pl.ds(row0 + half * TMD, TMD)],
                                        sem.at[half]) for half in range(2)]
        for cp in copies:
            cp.start()
        for cp in copies:
            cp.wait()

    @pl.when(i == 0)
    def _():
        hbuf[...] = jnp.zeros_like(hbuf)
        for e in range(NE):
            zero_tile(pl.multiple_of(zs_ref[e], TME))

        def tail(t, carry):
            zero_tile(pl.multiple_of(t * TME, TME))
            return carry
        lax.fori_loop(nu_ref[0], NT, tail, 0)

    @pl.when(i >= 2)
    def _():
        drain(slot)

    hbuf[slot] = _norm_rows(x_ref[...], g_ref[...], 1.0 + sc_ref[0], sh_ref[0])

    def issue(t, carry):
        tok = i * TMD + t
        src = hbuf.at[slot, pl.ds(t, 1)]
        pltpu.make_async_copy(src, xs_hbm.at[pl.ds(p0_ref[tok], 1)], sem.at[slot]).start()
        pltpu.make_async_copy(src, xs_hbm.at[pl.ds(p1_ref[tok], 1)], sem.at[slot]).start()
        return carry
    lax.fori_loop(0, TMD, issue, 0, unroll=8)

    @pl.when(i == n - 1)
    def _():
        drain(1 - slot)
        drain(slot)


def _dispatch(pos0, pos1, zstart, nu, x, g, sc, sh):
    assert TME == 2 * TMD
    return pl.pallas_call(
        _dispatch_kernel,
        grid_spec=pltpu.PrefetchScalarGridSpec(
            num_scalar_prefetch=4,
            grid=(T // TMD,),
            in_specs=[
                pl.BlockSpec((TMD, D), lambda i, *_: (i, 0)),
                pl.BlockSpec((1, D), lambda i, *_: (0, 0)),
                pl.BlockSpec((1, 1, D), lambda i, *_: (i * TMD // SEQ, 0, 0)),
                pl.BlockSpec((1, 1, D), lambda i, *_: (i * TMD // SEQ, 0, 0)),
            ],
            out_specs=pl.BlockSpec(memory_space=pl.ANY),
            scratch_shapes=[pltpu.VMEM((2, TMD, D), F32), pltpu.SemaphoreType.DMA((2,))],
        ),
        out_shape=jax.ShapeDtypeStruct((R, D), F32),
        compiler_params=_cp(("arbitrary",)),
        name="moe_dispatch",
    )(pos0, pos1, zstart, nu, x, g, sc, sh)


def _absmax(ref, rows):
    m = None
    for r0 in range(0, ref.shape[0], rows):
        c = jnp.max(jnp.abs(ref[r0:r0 + rows, :]), axis=0, keepdims=True)
        m = c if m is None else jnp.maximum(m, c)
    return jnp.max(m.astype(F32), axis=1, keepdims=True)


def _f8_scale(amax):
    return jnp.where(amax > 0.0, (0.5 * F8_MAX) / amax, 1.0)


def _to_f8(src_ref, dst_ref, scale, rows):
    for r0 in range(0, src_ref.shape[0], rows):
        dst_ref[r0:r0 + rows, :] = (src_ref[r0:r0 + rows, :].astype(F32) * scale).astype(F8)


def _splat(s):
    return jnp.broadcast_to(s, (8, LANES))


def _moe_up_kernel(te_ref, first_ref, nu_ref, xs_ref, wg_ref, wu_ref, o_ref, so_ref,
                   wg_scr, wu_scr, x_scr, sw_scr, sx_scr):
    j = pl.program_id(0)
    i = pl.program_id(1)

    @pl.when(first_ref[i] == 1)
    def _():
        for k, (w_ref, w_scr) in enumerate(((wg_ref, wg_scr), (wu_ref, wu_scr))):
            s = _f8_scale(_absmax(w_ref.at[0], 256))
            _to_f8(w_ref.at[0], w_scr, s, 256)
            sw_scr[k] = _splat(s)

    @pl.when((i < nu_ref[0]) & (j == 0))
    def _():
        sx_scr[i] = _splat(_f8_scale(_absmax(xs_ref, 128)))

    @pl.when(i < nu_ref[0])
    def _():
        sx = sx_scr[i][0:1, 0:1]
        _to_f8(xs_ref, x_scr, sx, 128)
        x = x_scr[...]
        gate = jnp.dot(x, wg_scr[...], preferred_element_type=F32) * (1.0 / (sx * sw_scr[0][0:1, 0:1]))
        up = jnp.dot(x, wu_scr[...], preferred_element_type=F32) * (1.0 / (sx * sw_scr[1][0:1, 0:1]))
        act = gate * jax.nn.sigmoid(gate) * up
        amax = jnp.max(jnp.max(jnp.abs(act), axis=0, keepdims=True), axis=1, keepdims=True)
        so = _f8_scale(amax)
        o_ref[...] = (act * so).astype(F8)
        so_ref[0, 0] = _splat(so)

    @pl.when(i >= nu_ref[0])
    def _():
        o_ref[...] = jnp.zeros_like(o_ref)
        so_ref[0, 0] = jnp.ones((8, LANES), F32)


def _moe_up(te, first, nu, xs, w_gu):
    nj = DFE // TNE1
    tile = lambda j, i, te, first, nu: jnp.maximum(jnp.minimum(i, nu[0] - 1), 0)
    return pl.pallas_call(
        _moe_up_kernel,
        grid_spec=pltpu.PrefetchScalarGridSpec(
            num_scalar_prefetch=3,
            grid=(nj, NT),
            in_specs=[
                pl.BlockSpec((TME, D), lambda *a: (tile(*a), 0)),
                pl.BlockSpec((1, D, TNE1), lambda j, i, te, first, nu: (te[i], 0, j)),
                pl.BlockSpec((1, D, TNE1), lambda j, i, te, first, nu: (te[i], 0, j + nj)),
            ],
            out_specs=[
                pl.BlockSpec((TME, TNE1), lambda j, i, te, first, nu: (i, j)),
                pl.BlockSpec((1, 1, 8, LANES), lambda j, i, te, first, nu: (i, j, 0, 0)),
            ],
            scratch_shapes=[
                pltpu.VMEM((D, TNE1), F8),
                pltpu.VMEM((D, TNE1), F8),
                pltpu.VMEM((TME, D), F8),
                pltpu.VMEM((2, 8, LANES), F32),
                pltpu.VMEM((NT, 8, LANES), F32),
            ],
        ),
        out_shape=[
            jax.ShapeDtypeStruct((R, DFE), F8),
            jax.ShapeDtypeStruct((NT, nj, 8, LANES), F32),
        ],
        compiler_params=_cp(("arbitrary", "arbitrary")),
        name="moe_up",
    )(te, first, nu, xs, w_gu, w_gu)


def _moe_down_kernel(te_ref, first_ref, nu_ref, a_ref, sa_ref, w_ref, o_ref, w_scr, sw_scr):
    i = pl.program_id(1)

    @pl.when(first_ref[i] == 1)
    def _():
        s = _f8_scale(_absmax(w_ref.at[0], 512))
        _to_f8(w_ref.at[0], w_scr, s, 512)
        sw_scr[...] = _splat(s)

    @pl.when(i < nu_ref[0])
    def _():
        sw = sw_scr[0:1, 0:1]
        acc = None
        for c in range(DFE // TNE1):
            ks = slice(c * TNE1, (c + 1) * TNE1)
            part = jnp.dot(a_ref[:, ks], w_scr[ks, :], preferred_element_type=F32)
            part = part * (1.0 / (sa_ref[0, c][0:1, 0:1] * sw))
            acc = part if acc is None else acc + part
        o_ref[...] = acc

    @pl.when(i >= nu_ref[0])
    def _():
        o_ref[...] = jnp.zeros_like(o_ref)


def _moe_down(te, first, nu, a, sa, w_d):
    tile = lambda j, i, te, first, nu: jnp.maximum(jnp.minimum(i, nu[0] - 1), 0)
    return pl.pallas_call(
        _moe_down_kernel,
        grid_spec=pltpu.PrefetchScalarGridSpec(
            num_scalar_prefetch=3,
            grid=(D // TNE2, NT),
            in_specs=[
                pl.BlockSpec((TME, DFE), lambda *a: (tile(*a), 0)),
                pl.BlockSpec((1, DFE // TNE1, 8, LANES), lambda *a: (tile(*a), 0, 0, 0)),
                pl.BlockSpec((1, DFE, TNE2), lambda j, i, te, first, nu: (te[i], 0, j)),
            ],
            out_specs=pl.BlockSpec((TME, TNE2), lambda j, i, te, first, nu: (i, j)),
            scratch_shapes=[
                pltpu.VMEM((DFE, TNE2), F8),
                pltpu.VMEM((8, LANES), F32),
            ],
        ),
        out_shape=jax.ShapeDtypeStruct((R, D), F32),
        compiler_params=_cp(("arbitrary", "arbitrary")),
        name="moe_down",
    )(te, first, nu, a, sa, w_d)


def _combine_kernel(p0_ref, p1_ref, ys_hbm, x_ref, route_ref, gate_ref, o_ref, buf, sem):
    i = pl.program_id(0)
    n = pl.num_programs(0)

    def row_copy(src, k, t, slot):
        return pltpu.make_async_copy(ys_hbm.at[pl.ds(src, 1)], buf.at[slot, k, pl.ds(t, 1)],
                                     sem.at[slot])

    def issue(step, slot):
        def body(t, carry):
            tok = step * TMC + t
            row_copy(p0_ref[tok], 0, t, slot).start()
            row_copy(p1_ref[tok], 1, t, slot).start()
            return carry
        lax.fori_loop(0, TMC, body, 0, unroll=8)

    def drain(slot):
        for k in range(2):
            pltpu.make_async_copy(ys_hbm.at[pl.ds(0, TMC)], buf.at[slot, k], sem.at[slot]).wait()

    @pl.when(i == 0)
    def _():
        issue(0, 0)

    @pl.when(i + 1 < n)
    def _():
        issue(i + 1, (i + 1) % 2)

    slot = i % 2
    drain(slot)
    w0 = route_ref[:, 4:5]
    w1 = route_ref[:, 5:6]
    y = w0 * buf[slot, 0] + w1 * buf[slot, 1]
    o_ref[...] = x_ref[...] + gate_ref[0] * y


def _combine(pos0, pos1, ys, x, route, gate):
    return pl.pallas_call(
        _combine_kernel,
        grid_spec=pltpu.PrefetchScalarGridSpec(
            num_scalar_prefetch=2,
            grid=(T // TMC,),
            in_specs=[
                pl.BlockSpec(memory_space=pl.ANY),
                pl.BlockSpec((TMC, D), lambda i, p0, p1: (i, 0)),
                pl.BlockSpec((TMC, LANES), lambda i, p0, p1: (i, 0)),
                pl.BlockSpec((1, 1, D), lambda i, p0, p1: (i * TMC // SEQ, 0, 0)),
            ],
            out_specs=pl.BlockSpec((TMC, D), lambda i, p0, p1: (i, 0)),
            scratch_shapes=[
                pltpu.VMEM((2, 2, TMC, D), F32),
                pltpu.SemaphoreType.DMA((2,)),
            ],
        ),
        out_shape=jax.ShapeDtypeStruct((T, D), F32),
        compiler_params=_cp(("arbitrary",)),
        name="moe_combine",
    )(pos0, pos1, ys, x, route, gate)


def _routing_plan(route, counts):
    e0 = route[:, 0].astype(jnp.int32)
    e1 = route[:, 1].astype(jnp.int32)
    r0 = route[:, 2].astype(jnp.int32)
    r1 = route[:, 3].astype(jnp.int32)
    cnt = counts[0, :NE].astype(jnp.int32)
    ntile = (cnt + TME - 1) // TME
    tile_end = jnp.cumsum(ntile)
    offs = (tile_end - ntile) * TME
    nu = tile_end[-1]
    tid = jnp.arange(NT, dtype=jnp.int32)
    te_raw = jnp.sum((tid[:, None] >= tile_end[None, :]).astype(jnp.int32), axis=1)
    te_last = jnp.sum((nu - 1 >= tile_end).astype(jnp.int32))
    te = jnp.where(tid < nu, te_raw, te_last).astype(jnp.int32)
    prev = jnp.concatenate([jnp.full((1,), -1, jnp.int32), te[:-1]])
    first = ((te != prev) & (tid < nu)).astype(jnp.int32)
    pos0 = offs[e0] + r0
    pos1 = offs[e1] + r1
    zstart = jnp.maximum(tile_end - 1, 0).astype(jnp.int32) * TME
    return pos0, pos1, te, first, nu.reshape(1).astype(jnp.int32), zstart


def kernel(x, c, positions, ada_w, ada_b, norm_g, gmlp_w_in, gmlp_b_in, gmlp_ln_g, gmlp_ln_b,
           gmlp_w_s, gmlp_b_s, gmlp_w_out, attn_w_qkv, attn_q_norm_g, attn_k_norm_g, attn_sinks,
           attn_w_o, ffn_w_gate_up, ffn_w_down, moe_router_w, moe_router_b, moe_w_gate_up,
           moe_w_down):
    xf = x.reshape(T, D)
    c_pad = jnp.concatenate([c, jnp.zeros((8 - BATCH, D), F32)], axis=0)
    mod = _ada_mod(c_pad, ada_w, ada_b)[:, :BATCH, :]

    def mods(layer):
        return [mod[layer, :, k * D:(k + 1) * D].reshape(BATCH, 1, D) for k in range(6)]

    sh1, sc1, g1, sh2, sc2, g2 = mods(0)
    z, st = _gmlp_in(xf, norm_g[0, 0].reshape(1, D), sc1, sh1, gmlp_w_in[0].astype(BF16),
                     gmlp_b_in[0])
    gated = _gmlp_gate(z, st, gmlp_ln_g[0], gmlp_ln_b[0], gmlp_w_s[0], gmlp_b_s[0])
    xf = _res_gemm(gated, gmlp_w_out[0].astype(BF16), xf, g1, tn=2 * TN, name="gmlp_out")
    act = _ffn_in(xf, norm_g[0, 1].reshape(1, D), sc2, sh2, ffn_w_gate_up[0].astype(BF16))
    xf = _res_gemm(act, ffn_w_down[0].astype(BF16), xf, g2, tn=TN, name="ffn_out")

    sh1, sc1, g1, sh2, sc2, g2 = mods(1)
    pos = positions.reshape(T, 1).astype(F32)
    inv_freq = THETA ** (-jnp.arange(0, ROT, 2, dtype=F32) / ROT)
    invf = jnp.tile(inv_freq, LANES // (ROT // 2)).reshape(1, LANES)
    scale = HD ** -0.5
    gain = jnp.concatenate([jnp.tile(attn_q_norm_g[0] * scale, NH),
                            jnp.tile(attn_k_norm_g[0], NKV),
                            jnp.ones((NKV * HD,), F32)]).reshape(1, QKV)
    flag = jnp.concatenate([jnp.ones(((NH + NKV) * HD,), F32),
                            jnp.zeros((NKV * HD,), F32)]).reshape(1, QKV)
    head_of_col = jnp.arange(TN, dtype=jnp.int32) // HD
    ind = (head_of_col[:, None] == jnp.arange(LANES, dtype=jnp.int32)[None, :]).astype(BF16)
    q5, k5, v5 = _qkv(xf, norm_g[1, 0].reshape(1, D), sc1, sh1, pos, invf,
                      attn_w_qkv[0].astype(BF16), gain, flag, ind, ind.T)
    sink5 = jnp.repeat(attn_sinks[0].astype(F32), BLK).reshape(NKV, 1, GQA * BLK)
    o = _attention(q5, k5, v5, sink5)
    xf = _res_gemm(o, attn_w_o[0].astype(BF16), xf, g1, tn=2 * TN, name="attn_out")

    rw_pad = jnp.concatenate([moe_router_w[0].astype(F32),
                              jnp.zeros((D, LANES - NE), F32)], axis=1)
    rb_pad = jnp.concatenate([moe_router_b[0].astype(F32),
                              jnp.zeros((LANES - NE,), F32)]).reshape(1, LANES)
    route, counts = _router(xf, norm_g[1, 1].reshape(1, D), sc2, sh2, rw_pad, rb_pad)
    pos0, pos1, te, first, nu, zstart = _routing_plan(route, counts)
    xs = _dispatch(pos0, pos1, zstart, nu, xf, norm_g[1, 1].reshape(1, D), sc2, sh2)
    hmid, hscale = _moe_up(te, first, nu, xs, moe_w_gate_up[0])
    ys = _moe_down(te, first, nu, hmid, hscale, moe_w_down[0])
    xf = _combine(pos0, pos1, ys, xf, route, g2)
    return xf.reshape(BATCH, SEQ, D)
```

```python
import functools
import math

import jax
import jax.numpy as jnp
from jax import lax
from jax.experimental import pallas as pl
from jax.experimental.pallas import tpu as pltpu

F32 = jnp.float32
BF16 = jnp.bfloat16
F8 = jnp.float8_e4m3fn
F8_MAX = 448.0

D = 2048
BATCH = 4
SEQ = 4096
T = BATCH * SEQ
EPS = 1e-6
MASK_VALUE = -1e30

CHUNK = 128
GW = 2 * D
GROUPS = 8
GDIM = GW // GROUPS

HD = 64
NH = D // HD
NKV = 4
GQA = NH // NKV
BLK = 128
NB = SEQ // BLK
ROT = 16
THETA = 500000.0
QKV = (NH + 2 * NKV) * HD

DFF = 5632
NE = 8
DFE = 7168

LANES = 128
VMEM_BYTES_V7X = 64 * 1024 * 1024
VMEM_LIMIT = VMEM_BYTES_V7X * 7 // 8

TM = 1024
TN = 512
TNG = 1024
TMG = 512
TMR = 512
TME = 512
TNE1 = 1024
TNE2 = 512
TMC = 256
TMD = 256
R = 2 * T + NE * TME
NT = R // TME


def _cp(sem):
    return pltpu.CompilerParams(dimension_semantics=sem, vmem_limit_bytes=VMEM_LIMIT)


def _ada_kernel(c_ref, w_ref, b_ref, o_ref):
    c = c_ref[...]
    sc = (c * jax.nn.sigmoid(c)).astype(BF16)
    o_ref[0] = jnp.dot(sc, w_ref[0].astype(BF16), preferred_element_type=F32) + b_ref[0]


def _ada_mod(c_pad, ada_w, ada_b):
    depth = ada_w.shape[0]
    n = ada_w.shape[2]
    tn = 1024
    return pl.pallas_call(
        _ada_kernel,
        grid=(depth, n // tn),
        in_specs=[
            pl.BlockSpec((8, D), lambda l, j: (0, 0)),
            pl.BlockSpec((1, D, tn), lambda l, j: (l, 0, j)),
            pl.BlockSpec((1, 1, tn), lambda l, j: (l, 0, j)),
        ],
        out_specs=pl.BlockSpec((1, 8, tn), lambda l, j: (l, 0, j)),
        out_shape=jax.ShapeDtypeStruct((depth, 8, n), F32),
        compiler_params=_cp(("arbitrary", "arbitrary")),
        name="ada_mod",
    )(c_pad, ada_w, ada_b.reshape(depth, 1, n))


def _norm_rows(x, g, a, b):
    ms = jnp.mean(x * x, axis=-1, keepdims=True)
    return (x * lax.rsqrt(ms + EPS) * g) * a + b


def _norm_prologue(x_ref, g_ref, sc_ref, sh_ref, h_ref, rows=256):
    g = g_ref[...]
    a = 1.0 + sc_ref[0]
    b = sh_ref[0]
    for r0 in range(0, x_ref.shape[0], rows):
        h_ref[r0:r0 + rows, :] = _norm_rows(x_ref[r0:r0 + rows, :], g, a, b).astype(BF16)


def _norm_specs(tm):
    return [
        pl.BlockSpec((tm, D), lambda i, j: (i, 0)),
        pl.BlockSpec((1, D), lambda i, j: (0, 0)),
        pl.BlockSpec((1, 1, D), lambda i, j: (i * tm // SEQ, 0, 0)),
        pl.BlockSpec((1, 1, D), lambda i, j: (i * tm // SEQ, 0, 0)),
    ]


def _gmlp_in_kernel(x_ref, g_ref, sc_ref, sh_ref, w_ref, b_ref, z_ref, st_ref,
                    h_scr, s1_scr, s2_scr, *, nj, jv0, nv):
    j = pl.program_id(1)

    @pl.when(j == 0)
    def _():
        _norm_prologue(x_ref, g_ref, sc_ref, sh_ref, h_scr)
        s1_scr[...] = jnp.zeros_like(s1_scr)
        s2_scr[...] = jnp.zeros_like(s2_scr)

    acc = jnp.dot(h_scr[...], w_ref[...], preferred_element_type=F32) + b_ref[...]
    z = 0.5 * acc * (1.0 + lax.erf(acc * (1.0 / math.sqrt(2.0))))
    z_ref[...] = z.astype(BF16)

    @pl.when(j >= jv0)
    def _():
        s1_scr[...] += jnp.sum(z, axis=-1, keepdims=True)
        s2_scr[...] += jnp.sum(z * z, axis=-1, keepdims=True)

    @pl.when(j == nj - 1)
    def _():
        mean = s1_scr[...] * (1.0 / nv)
        var = jnp.maximum(s2_scr[...] * (1.0 / nv) - mean * mean, 0.0)
        rstd = lax.rsqrt(var + EPS)
        lane = lax.broadcasted_iota(jnp.int32, st_ref.shape, 1)
        st_ref[...] = jnp.where(lane == 0, mean, jnp.where(lane == 1, rstd, 0.0))


def _gmlp_in(x, g, sc, sh, w, b):
    n = w.shape[1]
    nj = n // TNG
    kern = functools.partial(_gmlp_in_kernel, nj=nj, jv0=GW // TNG, nv=float(GW))
    return pl.pallas_call(
        kern,
        grid=(T // TM, nj),
        in_specs=_norm_specs(TM) + [
            pl.BlockSpec((D, TNG), lambda i, j: (0, j)),
            pl.BlockSpec((1, TNG), lambda i, j: (0, j)),
        ],
        out_specs=[
            pl.BlockSpec((TM, TNG), lambda i, j: (i, j)),
            pl.BlockSpec((TM, LANES), lambda i, j: (i, 0)),
        ],
        out_shape=[
            jax.ShapeDtypeStruct((T, n), BF16),
            jax.ShapeDtypeStruct((T, LANES), F32),
        ],
        scratch_shapes=[
            pltpu.VMEM((TM, D), BF16),
            pltpu.VMEM((TM, 1), F32),
            pltpu.VMEM((TM, 1), F32),
        ],
        compiler_params=_cp(("arbitrary", "arbitrary")),
        name="gmlp_in",
    )(x, g, sc, sh, w, b.reshape(1, n))


def _gmlp_gate_kernel(u_ref, v_ref, st_ref, lng_ref, lnb_ref, ws_ref, bs_ref, o_ref):
    row = lax.broadcasted_iota(jnp.int32, (CHUNK, CHUNK), 0)
    col = lax.broadcasted_iota(jnp.int32, (CHUNK, CHUNK), 1)
    causal = row >= col
    ws = [jnp.where(causal, ws_ref[g], 0.0).astype(BF16) for g in range(GROUPS)]
    for c in range(TMG // CHUNK):
        rs = slice(c * CHUNK, (c + 1) * CHUNK)
        mean = st_ref[rs, 0:1]
        rstd = st_ref[rs, 1:2]
        for g in range(GROUPS):
            cs = slice(g * GDIM, (g + 1) * GDIM)
            vn = (v_ref[rs, cs].astype(F32) - mean) * rstd * lng_ref[:, cs] + lnb_ref[:, cs]
            mixed = jnp.dot(ws[g], vn.astype(BF16), preferred_element_type=F32) + bs_ref[g]
            o_ref[rs, cs] = (u_ref[rs, cs].astype(F32) * mixed).astype(BF16)


def _gmlp_gate(z, st, ln_g, ln_b, w_s, b_s):
    return pl.pallas_call(
        _gmlp_gate_kernel,
        grid=(T // TMG,),
        in_specs=[
            pl.BlockSpec((TMG, GW), lambda i: (i, 0)),
            pl.BlockSpec((TMG, GW), lambda i: (i, 1)),
            pl.BlockSpec((TMG, LANES), lambda i: (i, 0)),
            pl.BlockSpec((1, GW), lambda i: (0, 0)),
            pl.BlockSpec((1, GW), lambda i: (0, 0)),
            pl.BlockSpec((GROUPS, CHUNK, CHUNK), lambda i: (0, 0, 0)),
            pl.BlockSpec((GROUPS, CHUNK, 1), lambda i: (0, 0, 0)),
        ],
        out_specs=pl.BlockSpec((TMG, GW), lambda i: (i, 0)),
        out_shape=jax.ShapeDtypeStruct((T, GW), BF16),
        compiler_params=_cp(("arbitrary",)),
        name="gmlp_gate",
    )(z, z, st, ln_g.reshape(1, GW), ln_b.reshape(1, GW), w_s, b_s.reshape(GROUPS, CHUNK, 1))


def _res_gemm_kernel(a_ref, w_ref, x_ref, gate_ref, o_ref):
    y = jnp.dot(a_ref[...], w_ref[...], preferred_element_type=F32)
    o_ref[...] = x_ref[...] + gate_ref[0] * y


def _res_gemm(a, w, x, gate, *, tn, name):
    k = a.shape[1]
    return pl.pallas_call(
        _res_gemm_kernel,
        grid=(T // TM, D // tn),
        in_specs=[
            pl.BlockSpec((TM, k), lambda i, j: (i, 0)),
            pl.BlockSpec((k, tn), lambda i, j: (0, j)),
            pl.BlockSpec((TM, tn), lambda i, j: (i, j)),
            pl.BlockSpec((1, 1, tn), lambda i, j: (i * TM // SEQ, 0, j)),
        ],
        out_specs=pl.BlockSpec((TM, tn), lambda i, j: (i, j)),
        out_shape=jax.ShapeDtypeStruct((T, D), F32),
        compiler_params=_cp(("arbitrary", "arbitrary")),
        name=name,
    )(a, w, x, gate)


def _ffn_in_kernel(x_ref, g_ref, sc_ref, sh_ref, wg_ref, wu_ref, o_ref, h_scr):
    @pl.when(pl.program_id(1) == 0)
    def _():
        _norm_prologue(x_ref, g_ref, sc_ref, sh_ref, h_scr)

    h = h_scr[...]
    gate = jnp.dot(h, wg_ref[...], preferred_element_type=F32)
    up = jnp.dot(h, wu_ref[...], preferred_element_type=F32)
    o_ref[...] = (gate * jax.nn.sigmoid(gate) * up).astype(BF16)


def _ffn_in(x, g, sc, sh, w_gu):
    nj = DFF // TN
    return pl.pallas_call(
        _ffn_in_kernel,
        grid=(T // TM, nj),
        in_specs=_norm_specs(TM) + [
            pl.BlockSpec((D, TN), lambda i, j: (0, j)),
            pl.BlockSpec((D, TN), lambda i, j: (0, j + nj)),
        ],
        out_specs=pl.BlockSpec((TM, TN), lambda i, j: (i, j)),
        out_shape=jax.ShapeDtypeStruct((T, DFF), BF16),
        scratch_shapes=[pltpu.VMEM((TM, D), BF16)],
        compiler_params=_cp(("arbitrary", "arbitrary")),
        name="ffn_in",
    )(x, g, sc, sh, w_gu, w_gu)


def _qkv_kernel(x_ref, g_ref, sc_ref, sh_ref, pos_ref, invf_ref, w_ref, gain_ref, flag_ref,
                ind_ref, indt_ref, q_ref, k_ref, v_ref, h_scr, ct_scr, s1_scr, s2_scr):
    @pl.when(pl.program_id(1) == 0)
    def _():
        _norm_prologue(x_ref, g_ref, sc_ref, sh_ref, h_scr)
        lane = lax.broadcasted_iota(jnp.int32, ct_scr.shape, 1)
        m = lane & (HD - 1)
        ang = pos_ref[...] * invf_ref[...]
        c = jnp.cos(ang)
        s = jnp.sin(ang)
        ct_scr[...] = jnp.where(m < ROT, c, 1.0)
        s1_scr[...] = jnp.where(m < ROT // 2, -s, 0.0)
        s2_scr[...] = jnp.where((m >= ROT // 2) & (m < ROT), s, 0.0)

    y = jnp.dot(h_scr[...], w_ref[...], preferred_element_type=F32)
    ss = jnp.dot((y * y).astype(BF16), ind_ref[...], preferred_element_type=F32)
    r = lax.rsqrt(ss * (1.0 / HD) + EPS)
    rhi = r.astype(BF16)
    rlo = (r - rhi.astype(F32)).astype(BF16)
    indt = indt_ref[...]
    rb = (jnp.dot(rhi, indt, preferred_element_type=F32)
          + jnp.dot(rlo, indt, preferred_element_type=F32))
    yn = y * rb * gain_ref[...]
    ct = ct_scr[...]
    s1 = s1_scr[...]
    s2 = s2_scr[...]
    parts = []
    for k in range(TN // LANES):
        yb = yn[:, k * LANES:(k + 1) * LANES]
        parts.append(yb * ct + pltpu.roll(yb, LANES - ROT // 2, 1) * s1
                     + pltpu.roll(yb, ROT // 2, 1) * s2)
    yr = jnp.concatenate(parts, axis=1)
    res = jnp.where(flag_ref[...] > 0.5, yr, y).astype(BF16)

    j = pl.program_id(1)

    @pl.when(j < NKV)
    def _():
        for blk in range(TM // BLK):
            rs = slice(blk * BLK, (blk + 1) * BLK)
            for gq in range(GQA):
                q_ref[blk, 0, gq * BLK:(gq + 1) * BLK, :] = res[rs, gq * HD:(gq + 1) * HD]

    @pl.when(j == NKV)
    def _():
        for blk in range(TM // BLK):
            rs = slice(blk * BLK, (blk + 1) * BLK)
            for kv in range(NKV):
                k_ref[blk, kv] = res[rs, kv * HD:(kv + 1) * HD]
                v_ref[blk, kv] = res[rs, (NKV + kv) * HD:(NKV + kv + 1) * HD]


def _qkv(x, g, sc, sh, pos, invf, w, gain, flag, ind, indt):
    nblk = TM // BLK
    return pl.pallas_call(
        _qkv_kernel,
        grid=(T // TM, QKV // TN),
        in_specs=_norm_specs(TM) + [
            pl.BlockSpec((TM, 1), lambda i, j: (i, 0)),
            pl.BlockSpec((1, LANES), lambda i, j: (0, 0)),
            pl.BlockSpec((D, TN), lambda i, j: (0, j)),
            pl.BlockSpec((1, TN), lambda i, j: (0, j)),
            pl.BlockSpec((1, TN), lambda i, j: (0, j)),
            pl.BlockSpec((TN, LANES), lambda i, j: (0, 0)),
            pl.BlockSpec((LANES, TN), lambda i, j: (0, 0)),
        ],
        out_specs=[
            pl.BlockSpec((nblk, 1, GQA * BLK, HD), lambda i, j: (i, jnp.minimum(j, NKV - 1), 0, 0)),
            pl.BlockSpec((nblk, NKV, BLK, HD), lambda i, j: (i, 0, 0, 0)),
            pl.BlockSpec((nblk, NKV, BLK, HD), lambda i, j: (i, 0, 0, 0)),
        ],
        out_shape=[
            jax.ShapeDtypeStruct((BATCH * NB, NKV, GQA * BLK, HD), BF16),
            jax.ShapeDtypeStruct((BATCH * NB, NKV, BLK, HD), BF16),
            jax.ShapeDtypeStruct((BATCH * NB, NKV, BLK, HD), BF16),
        ],
        scratch_shapes=[
            pltpu.VMEM((TM, D), BF16),
            pltpu.VMEM((TM, LANES), F32),
            pltpu.VMEM((TM, LANES), F32),
            pltpu.VMEM((TM, LANES), F32),
        ],
        compiler_params=_cp(("arbitrary", "arbitrary")),
        name="qkv",
    )(x, g, sc, sh, pos, invf, w, gain, flag, ind, indt)


def _attn_kernel(q_ref, kp_ref, kc_ref, vp_ref, vc_ref, sink_ref, o_ref, bias_scr):
    n = pl.program_id(1)
    ri = lax.broadcasted_iota(jnp.int32, (2 * BLK, BLK), 0)
    qi = lax.broadcasted_iota(jnp.int32, (2 * BLK, BLK), 1)
    valid = (ri > qi) & (ri <= qi + BLK) & ((ri >= BLK) | (n > 0))
    bias_scr[...] = jnp.where(valid, 0.0, MASK_VALUE)
    for kv in range(NKV):
        q = q_ref[0, kv]
        kw = jnp.concatenate([kp_ref[0, kv], kc_ref[0, kv]], axis=0)
        vw = jnp.concatenate([vp_ref[0, kv], vc_ref[0, kv]], axis=0)
        st = lax.dot_general(kw, q, (((1,), (1,)), ((), ())), preferred_element_type=F32)
        st = st + jnp.concatenate([bias_scr[...]] * GQA, axis=1)
        sink = sink_ref[kv]
        m = jnp.maximum(jnp.max(st, axis=0, keepdims=True), sink)
        p = jnp.exp(st - m)
        den = jnp.sum(p, axis=0, keepdims=True) + jnp.exp(sink - m)
        ot = lax.dot_general(vw, p.astype(BF16), (((0,), (0,)), ((), ())),
                             preferred_element_type=F32)
        ot = ot * (1.0 / den)
        for gq in range(GQA):
            head = kv * GQA + gq
            o_ref[:, head * HD:(head + 1) * HD] = ot[:, gq * BLK:(gq + 1) * BLK].T.astype(BF16)


def _attention(q5, k5, v5, sink5):
    cur = lambda b, n: (b * NB + n, 0, 0, 0)
    prev = lambda b, n: (b * NB + jnp.maximum(n - 1, 0), 0, 0, 0)
    return pl.pallas_call(
        _attn_kernel,
        grid=(BATCH, NB),
        in_specs=[
            pl.BlockSpec((1, NKV, GQA * BLK, HD), cur),
            pl.BlockSpec((1, NKV, BLK, HD), prev),
            pl.BlockSpec((1, NKV, BLK, HD), cur),
            pl.BlockSpec((1, NKV, BLK, HD), prev),
            pl.BlockSpec((1, NKV, BLK, HD), cur),
            pl.BlockSpec((NKV, 1, GQA * BLK), lambda b, n: (0, 0, 0)),
        ],
        out_specs=pl.BlockSpec((BLK, NH * HD), lambda b, n: (b * NB + n, 0)),
        out_shape=jax.ShapeDtypeStruct((T, NH * HD), BF16),
        scratch_shapes=[pltpu.VMEM((2 * BLK, BLK), F32)],
        compiler_params=_cp(("arbitrary", "arbitrary")),
        name="attention",
    )(q5, k5, k5, v5, v5, sink5)


def _router_kernel(x_ref, g_ref, sc_ref, sh_ref, rw_ref, rb_ref, route_ref, cnt_ref, run_scr):
    i = pl.program_id(0)

    @pl.when(i == 0)
    def _():
        run_scr[...] = jnp.zeros_like(run_scr)

    h = _norm_rows(x_ref[...], g_ref[...], 1.0 + sc_ref[0], sh_ref[0])

    rw = rw_ref[...]
    h_hi = h.astype(BF16)
    h_lo = (h - h_hi.astype(F32)).astype(BF16)
    w_hi = rw.astype(BF16)
    w_lo = (rw - w_hi.astype(F32)).astype(BF16)
    logits = (jnp.dot(h_hi, w_hi, preferred_element_type=F32)
              + jnp.dot(h_lo, w_hi, preferred_element_type=F32)
              + jnp.dot(h_hi, w_lo, preferred_element_type=F32)) + rb_ref[...]
    lane = lax.broadcasted_iota(jnp.int32, logits.shape, 1).astype(F32)
    neg = -jnp.inf
    lg = jnp.where(lane < NE, logits, neg)
    v0 = jnp.max(lg, axis=-1, keepdims=True)
    i0 = jnp.min(jnp.where(lg == v0, lane, float(LANES)), axis=-1, keepdims=True)
    lg2 = jnp.where(lane == i0, neg, lg)
    v1 = jnp.max(lg2, axis=-1, keepdims=True)
    i1 = jnp.min(jnp.where(lg2 == v1, lane, float(LANES)), axis=-1, keepdims=True)
    t = jnp.exp(v1 - v0)
    w0 = 1.0 / (1.0 + t)
    w1 = t / (1.0 + t)

    sel0 = lane == i0
    sel1 = lane == i1
    onehot = jnp.where(sel0 | sel1, 1.0, 0.0)
    tm = onehot.shape[0]
    rr = lax.broadcasted_iota(jnp.int32, (tm, tm), 0)
    cc = lax.broadcasted_iota(jnp.int32, (tm, tm), 1)
    lower = jnp.where(rr > cc, 1.0, 0.0).astype(BF16)
    before = jnp.dot(lower, onehot.astype(BF16), preferred_element_type=F32) + run_scr[...]
    rank0 = jnp.sum(jnp.where(sel0, before, 0.0), axis=-1, keepdims=True)
    rank1 = jnp.sum(jnp.where(sel1, before, 0.0), axis=-1, keepdims=True)
    run_scr[...] += jnp.sum(onehot, axis=0, keepdims=True)
    cnt_ref[...] = run_scr[...]

    route_ref[...] = jnp.where(
        lane == 0, i0, jnp.where(
            lane == 1, i1, jnp.where(
                lane == 2, rank0, jnp.where(
                    lane == 3, rank1, jnp.where(
                        lane == 4, w0, jnp.where(lane == 5, w1, 0.0))))))


def _router(x, g, sc, sh, rw_pad, rb_pad):
    return pl.pallas_call(
        _router_kernel,
        grid=(T // TMR,),
        in_specs=[
            pl.BlockSpec((TMR, D), lambda i: (i, 0)),
            pl.BlockSpec((1, D), lambda i: (0, 0)),
            pl.BlockSpec((1, 1, D), lambda i: (i * TMR // SEQ, 0, 0)),
            pl.BlockSpec((1, 1, D), lambda i: (i * TMR // SEQ, 0, 0)),
            pl.BlockSpec((D, LANES), lambda i: (0, 0)),
            pl.BlockSpec((1, LANES), lambda i: (0, 0)),
        ],
        out_specs=[
            pl.BlockSpec((TMR, LANES), lambda i: (i, 0)),
            pl.BlockSpec((1, LANES), lambda i: (0, 0)),
        ],
        out_shape=[
            jax.ShapeDtypeStruct((T, LANES), F32),
            jax.ShapeDtypeStruct((1, LANES), F32),
        ],
        scratch_shapes=[pltpu.VMEM((1, LANES), F32)],
        compiler_params=_cp(("arbitrary",)),
        name="router",
    )(x, g, sc, sh, rw_pad, rb_pad)


def _dispatch_kernel(p0_ref, p1_ref, zs_ref, nu_ref, x_ref, g_ref, sc_ref, sh_ref, xs_hbm,
                     hbuf, sem):
    i = pl.program_id(0)
    n = pl.num_programs(0)
    slot = i % 2

    def drain(s):
        for _ in range(2):
            pltpu.make_async_copy(hbuf.at[s], xs_hbm.at[pl.ds(0, TMD)], sem.at[s]).wait()

    def zero_tile(row0):
        copies = [pltpu.make_async_copy(hbuf.at[half], xs_hbm.at[pl.ds(row0 + half * TMD, TMD)],
                                        sem.at[half]) for half in range(2)]
        for cp in copies:
            cp.start()
        for cp in copies:
            cp.wait()

    @pl.when(i == 0)
    def _():
        hbuf[...] = jnp.zeros_like(hbuf)
        for e in range(NE):
            zero_tile(pl.multiple_of(zs_ref[e], TME))

        def tail(t, carry):
            zero_tile(pl.multiple_of(t * TME, TME))
            return carry
        lax.fori_loop(nu_ref[0], NT, tail, 0)

    @pl.when(i >= 2)
    def _():
        drain(slot)

    hbuf[slot] = _norm_rows(x_ref[...], g_ref[...], 1.0 + sc_ref[0], sh_ref[0])

    def issue(t, carry):
        tok = i * TMD + t
        src = hbuf.at[slot, pl.ds(t, 1)]
        pltpu.make_async_copy(src, xs_hbm.at[pl.ds(p0_ref[tok], 1)], sem.at[slot]).start()
        pltpu.make_async_copy(src, xs_hbm.at[pl.ds(p1_ref[tok], 1)], sem.at[slot]).start()
        return carry
    lax.fori_loop(0, TMD, issue, 0, unroll=8)

    @pl.when(i == n - 1)
    def _():
        drain(1 - slot)
        drain(slot)


def _dispatch(pos0, pos1, zstart, nu, x, g, sc, sh):
    assert TME == 2 * TMD
    return pl.pallas_call(
        _dispatch_kernel,
        grid_spec=pltpu.PrefetchScalarGridSpec(
            num_scalar_prefetch=4,
            grid=(T // TMD,),
            in_specs=[
                pl.BlockSpec((TMD, D), lambda i, *_: (i, 0)),
                pl.BlockSpec((1, D), lambda i, *_: (0, 0)),
                pl.BlockSpec((1, 1, D), lambda i, *_: (i * TMD // SEQ, 0, 0)),
                pl.BlockSpec((1, 1, D), lambda i, *_: (i * TMD // SEQ, 0, 0)),
            ],
            out_specs=pl.BlockSpec(memory_space=pl.ANY),
            scratch_shapes=[pltpu.VMEM((2, TMD, D), F32), pltpu.SemaphoreType.DMA((2,))],
        ),
        out_shape=jax.ShapeDtypeStruct((R, D), F32),
        compiler_params=_cp(("arbitrary",)),
        name="moe_dispatch",
    )(pos0, pos1, zstart, nu, x, g, sc, sh)


def _absmax(ref, rows):
    m = None
    for r0 in range(0, ref.shape[0], rows):
        c = jnp.max(jnp.abs(ref[r0:r0 + rows, :]), axis=0, keepdims=True)
        m = c if m is None else jnp.maximum(m, c)
    return jnp.max(m.astype(F32), axis=1, keepdims=True)


def _f8_scale(amax):
    return jnp.where(amax > 0.0, (0.5 * F8_MAX) / amax, 1.0)


def _to_f8(src_ref, dst_ref, scale, rows):
    for r0 in range(0, src_ref.shape[0], rows):
        dst_ref[r0:r0 + rows, :] = (src_ref[r0:r0 + rows, :].astype(F32) * scale).astype(F8)


def _splat(s):
    return jnp.broadcast_to(s, (8, LANES))


def _moe_up_kernel(te_ref, first_ref, nu_ref, xs_ref, wg_ref, wu_ref, o_ref, so_ref,
                   wg_scr, wu_scr, x_scr, sw_scr, sx_scr):
    j = pl.program_id(0)
    i = pl.program_id(1)

    @pl.when(first_ref[i] == 1)
    def _():
        for k, (w_ref, w_scr) in enumerate(((wg_ref, wg_scr), (wu_ref, wu_scr))):
            s = _f8_scale(_absmax(w_ref.at[0], 256))
            _to_f8(w_ref.at[0], w_scr, s, 256)
            sw_scr[k] = _splat(s)

    @pl.when((i < nu_ref[0]) & (j == 0))
    def _():
        sx_scr[i] = _splat(_f8_scale(_absmax(xs_ref, 128)))

    @pl.when(i < nu_ref[0])
    def _():
        sx = sx_scr[i][0:1, 0:1]
        _to_f8(xs_ref, x_scr, sx, 128)
        x = x_scr[...]
        gate = jnp.dot(x, wg_scr[...], preferred_element_type=F32) * (1.0 / (sx * sw_scr[0][0:1, 0:1]))
        up = jnp.dot(x, wu_scr[...], preferred_element_type=F32) * (1.0 / (sx * sw_scr[1][0:1, 0:1]))
        act = gate * jax.nn.sigmoid(gate) * up
        amax = jnp.max(jnp.max(jnp.abs(act), axis=0, keepdims=True), axis=1, keepdims=True)
        so = _f8_scale(amax)
        o_ref[...] = (act * so).astype(F8)
        so_ref[0, 0] = _splat(so)

    @pl.when(i >= nu_ref[0])
    def _():
        o_ref[...] = jnp.zeros_like(o_ref)
        so_ref[0, 0] = jnp.ones((8, LANES), F32)


def _moe_up(te, first, nu, xs, w_gu):
    nj = DFE // TNE1
    tile = lambda j, i, te, first, nu: jnp.maximum(jnp.minimum(i, nu[0] - 1), 0)
    return pl.pallas_call(
        _moe_up_kernel,
        grid_spec=pltpu.PrefetchScalarGridSpec(
            num_scalar_prefetch=3,
            grid=(nj, NT),
            in_specs=[
                pl.BlockSpec((TME, D), lambda *a: (tile(*a), 0)),
                pl.BlockSpec((1, D, TNE1), lambda j, i, te, first, nu: (te[i], 0, j)),
                pl.BlockSpec((1, D, TNE1), lambda j, i, te, first, nu: (te[i], 0, j + nj)),
            ],
            out_specs=[
                pl.BlockSpec((TME, TNE1), lambda j, i, te, first, nu: (i, j)),
                pl.BlockSpec((1, 1, 8, LANES), lambda j, i, te, first, nu: (i, j, 0, 0)),
            ],
            scratch_shapes=[
                pltpu.VMEM((D, TNE1), F8),
                pltpu.VMEM((D, TNE1), F8),
                pltpu.VMEM((TME, D), F8),
                pltpu.VMEM((2, 8, LANES), F32),
                pltpu.VMEM((NT, 8, LANES), F32),
            ],
        ),
        out_shape=[
            jax.ShapeDtypeStruct((R, DFE), F8),
            jax.ShapeDtypeStruct((NT, nj, 8, LANES), F32),
        ],
        compiler_params=_cp(("arbitrary", "arbitrary")),
        name="moe_up",
    )(te, first, nu, xs, w_gu, w_gu)


def _moe_down_kernel(te_ref, first_ref, nu_ref, a_ref, sa_ref, w_ref, o_ref, w_scr, sw_scr):
    i = pl.program_id(1)

    @pl.when(first_ref[i] == 1)
    def _():
        s = _f8_scale(_absmax(w_ref.at[0], 512))
        _to_f8(w_ref.at[0], w_scr, s, 512)
        sw_scr[...] = _splat(s)

    @pl.when(i < nu_ref[0])
    def _():
        sw = sw_scr[0:1, 0:1]
        acc = None
        for c in range(DFE // TNE1):
            ks = slice(c * TNE1, (c + 1) * TNE1)
            part = jnp.dot(a_ref[:, ks], w_scr[ks, :], preferred_element_type=F32)
            part = part * (1.0 / (sa_ref[0, c][0:1, 0:1] * sw))
            acc = part if acc is None else acc + part
        o_ref[...] = acc

    @pl.when(i >= nu_ref[0])
    def _():
        o_ref[...] = jnp.zeros_like(o_ref)


def _moe_down(te, first, nu, a, sa, w_d):
    tile = lambda j, i, te, first, nu: jnp.maximum(jnp.minimum(i, nu[0] - 1), 0)
    return pl.pallas_call(
        _moe_down_kernel,
        grid_spec=pltpu.PrefetchScalarGridSpec(
            num_scalar_prefetch=3,
            grid=(D // TNE2, NT),
            in_specs=[
                pl.BlockSpec((TME, DFE), lambda *a: (tile(*a), 0)),
                pl.BlockSpec((1, DFE // TNE1, 8, LANES), lambda *a: (tile(*a), 0, 0, 0)),
                pl.BlockSpec((1, DFE, TNE2), lambda j, i, te, first, nu: (te[i], 0, j)),
            ],
            out_specs=pl.BlockSpec((TME, TNE2), lambda j, i, te, first, nu: (i, j)),
            scratch_shapes=[
                pltpu.VMEM((DFE, TNE2), F8),
                pltpu.VMEM((8, LANES), F32),
            ],
        ),
        out_shape=jax.ShapeDtypeStruct((R, D), F32),
        compiler_params=_cp(("arbitrary", "arbitrary")),
        name="moe_down",
    )(te, first, nu, a, sa, w_d)


def _combine_kernel(p0_ref, p1_ref, ys_hbm, x_ref, route_ref, gate_ref, o_ref, buf, sem):
    i = pl.program_id(0)
    n = pl.num_programs(0)

    def row_copy(src, k, t, slot):
        return pltpu.make_async_copy(ys_hbm.at[pl.ds(src, 1)], buf.at[slot, k, pl.ds(t, 1)],
                                     sem.at[slot])

    def issue(step, slot):
        def body(t, carry):
            tok = step * TMC + t
            row_copy(p0_ref[tok], 0, t, slot).start()
            row_copy(p1_ref[tok], 1, t, slot).start()
            return carry
        lax.fori_loop(0, TMC, body, 0, unroll=8)

    def drain(slot):
        for k in range(2):
            pltpu.make_async_copy(ys_hbm.at[pl.ds(0, TMC)], buf.at[slot, k], sem.at[slot]).wait()

    @pl.when(i == 0)
    def _():
        issue(0, 0)

    @pl.when(i + 1 < n)
    def _():
        issue(i + 1, (i + 1) % 2)

    slot = i % 2
    drain(slot)
    w0 = route_ref[:, 4:5]
    w1 = route_ref[:, 5:6]
    y = w0 * buf[slot, 0] + w1 * buf[slot, 1]
    o_ref[...] = x_ref[...] + gate_ref[0] * y


def _combine(pos0, pos1, ys, x, route, gate):
    return pl.pallas_call(
        _combine_kernel,
        grid_spec=pltpu.PrefetchScalarGridSpec(
            num_scalar_prefetch=2,
            grid=(T // TMC,),
            in_specs=[
                pl.BlockSpec(memory_space=pl.ANY),
                pl.BlockSpec((TMC, D), lambda i, p0, p1: (i, 0)),
                pl.BlockSpec((TMC, LANES), lambda i, p0, p1: (i, 0)),
                pl.BlockSpec((1, 1, D), lambda i, p0, p1: (i * TMC // SEQ, 0, 0)),
            ],
            out_specs=pl.BlockSpec((TMC, D), lambda i, p0, p1: (i, 0)),
            scratch_shapes=[
                pltpu.VMEM((2, 2, TMC, D), F32),
                pltpu.SemaphoreType.DMA((2,)),
            ],
        ),
        out_shape=jax.ShapeDtypeStruct((T, D), F32),
        compiler_params=_cp(("arbitrary",)),
        name="moe_combine",
    )(pos0, pos1, ys, x, route, gate)


def _routing_plan(route, counts):
    e0 = route[:, 0].astype(jnp.int32)
    e1 = route[:, 1].astype(jnp.int32)
    r0 = route[:, 2].astype(jnp.int32)
    r1 = route[:, 3].astype(jnp.int32)
    cnt = counts[0, :NE].astype(jnp.int32)
    ntile = (cnt + TME - 1) // TME
    tile_end = jnp.cumsum(ntile)
    offs = (tile_end - ntile) * TME
    nu = tile_end[-1]
    tid = jnp.arange(NT, dtype=jnp.int32)
    te_raw = jnp.sum((tid[:, None] >= tile_end[None, :]).astype(jnp.int32), axis=1)
    te_last = jnp.sum((nu - 1 >= tile_end).astype(jnp.int32))
    te = jnp.where(tid < nu, te_raw, te_last).astype(jnp.int32)
    prev = jnp.concatenate([jnp.full((1,), -1, jnp.int32), te[:-1]])
    first = ((te != prev) & (tid < nu)).astype(jnp.int32)
    pos0 = offs[e0] + r0
    pos1 = offs[e1] + r1
    zstart = jnp.maximum(tile_end - 1, 0).astype(jnp.int32) * TME
    return pos0, pos1, te, first, nu.reshape(1).astype(jnp.int32), zstart


def kernel(x, c, positions, ada_w, ada_b, norm_g, gmlp_w_in, gmlp_b_in, gmlp_ln_g, gmlp_ln_b,
           gmlp_w_s, gmlp_b_s, gmlp_w_out, attn_w_qkv, attn_q_norm_g, attn_k_norm_g, attn_sinks,
           attn_w_o, ffn_w_gate_up, ffn_w_down, moe_router_w, moe_router_b, moe_w_gate_up,
           moe_w_down):
    xf = x.reshape(T, D)
    c_pad = jnp.concatenate([c, jnp.zeros((8 - BATCH, D), F32)], axis=0)
    mod = _ada_mod(c_pad, ada_w, ada_b)[:, :BATCH, :]

    def mods(layer):
        return [mod[layer, :, k * D:(k + 1) * D].reshape(BATCH, 1, D) for k in range(6)]

    sh1, sc1, g1, sh2, sc2, g2 = mods(0)
    z, st = _gmlp_in(xf, norm_g[0, 0].reshape(1, D), sc1, sh1, gmlp_w_in[0].astype(BF16),
                     gmlp_b_in[0])
    gated = _gmlp_gate(z, st, gmlp_ln_g[0], gmlp_ln_b[0], gmlp_w_s[0], gmlp_b_s[0])
    xf = _res_gemm(gated, gmlp_w_out[0].astype(BF16), xf, g1, tn=2 * TN, name="gmlp_out")
    act = _ffn_in(xf, norm_g[0, 1].reshape(1, D), sc2, sh2, ffn_w_gate_up[0].astype(BF16))
    xf = _res_gemm(act, ffn_w_down[0].astype(BF16), xf, g2, tn=TN, name="ffn_out")

    sh1, sc1, g1, sh2, sc2, g2 = mods(1)
    pos = positions.reshape(T, 1).astype(F32)
    inv_freq = THETA ** (-jnp.arange(0, ROT, 2, dtype=F32) / ROT)
    invf = jnp.tile(inv_freq, LANES // (ROT // 2)).reshape(1, LANES)
    scale = HD ** -0.5
    gain = jnp.concatenate([jnp.tile(attn_q_norm_g[0] * scale, NH),
                            jnp.tile(attn_k_norm_g[0], NKV),
                            jnp.ones((NKV * HD,), F32)]).reshape(1, QKV)
    flag = jnp.concatenate([jnp.ones(((NH + NKV) * HD,), F32),
                            jnp.zeros((NKV * HD,), F32)]).reshape(1, QKV)
    head_of_col = jnp.arange(TN, dtype=jnp.int32) // HD
    ind = (head_of_col[:, None] == jnp.arange(LANES, dtype=jnp.int32)[None, :]).astype(BF16)
    q5, k5, v5 = _qkv(xf, norm_g[1, 0].reshape(1, D), sc1, sh1, pos, invf,
                      attn_w_qkv[0].astype(BF16), gain, flag, ind, ind.T)
    sink5 = jnp.repeat(attn_sinks[0].astype(F32), BLK).reshape(NKV, 1, GQA * BLK)
    o = _attention(q5, k5, v5, sink5)
    xf = _res_gemm(o, attn_w_o[0].astype(BF16), xf, g1, tn=2 * TN, name="attn_out")

    rw_pad = jnp.concatenate([moe_router_w[0].astype(F32),
                              jnp.zeros((D, LANES - NE), F32)], axis=1)
    rb_pad = jnp.concatenate([moe_router_b[0].astype(F32),
                              jnp.zeros((LANES - NE,), F32)]).reshape(1, LANES)
    route, counts = _router(xf, norm_g[1, 1].reshape(1, D), sc2, sh2, rw_pad, rb_pad)
    pos0, pos1, te, first, nu, zstart = _routing_plan(route, counts)
    xs = _dispatch(pos0, pos1, zstart, nu, xf, norm_g[1, 1].reshape(1, D), sc2, sh2)
    hmid, hscale = _moe_up(te, first, nu, xs, moe_w_gate_up[0])
    ys = _moe_down(te, first, nu, hmid, hscale, moe_w_down[0])
    xf = _combine(pos0, pos1, ys, xf, route, g2)
    return xf.reshape(BATCH, SEQ, D)
```

```python
import functools
import math

import jax
import jax.numpy as jnp
from jax import lax
from jax.experimental import pallas as pl
from jax.experimental.pallas import tpu as pltpu

F32 = jnp.float32
BF16 = jnp.bfloat16
F8 = jnp.float8_e4m3fn
F8_MAX = 448.0

D = 2048
BATCH = 4
SEQ = 4096
T = BATCH * SEQ
EPS = 1e-6
MASK_VALUE = -1e30

CHUNK = 128
GW = 2 * D
GROUPS = 8
GDIM = GW // GROUPS

HD = 64
NH = D // HD
NKV = 4
GQA = NH // NKV
BLK = 128
NB = SEQ // BLK
ROT = 16
THETA = 500000.0
QKV = (NH + 2 * NKV) * HD

DFF = 5632
NE = 8
DFE = 7168

LANES = 128
VMEM_BYTES_V7X = 64 * 1024 * 1024
VMEM_LIMIT = VMEM_BYTES_V7X * 7 // 8

TM = 1024
TN = 512
TNG = 1024
TMG = 512
TMR = 512
TME = 512
TNE1 = 1024
TNE2 = 512
TMC = 256
TMD = 256
R = 2 * T + NE * TME
NT = R // TME


def _cp(sem):
    return pltpu.CompilerParams(dimension_semantics=sem, vmem_limit_bytes=VMEM_LIMIT)


def _ada_kernel(c_ref, w_ref, b_ref, o_ref):
    c = c_ref[...]
    sc = (c * jax.nn.sigmoid(c)).astype(BF16)
    o_ref[0] = jnp.dot(sc, w_ref[0].astype(BF16), preferred_element_type=F32) + b_ref[0]


def _ada_mod(c_pad, ada_w, ada_b):
    depth = ada_w.shape[0]
    n = ada_w.shape[2]
    tn = 1024
    return pl.pallas_call(
        _ada_kernel,
        grid=(depth, n // tn),
        in_specs=[
            pl.BlockSpec((8, D), lambda l, j: (0, 0)),
            pl.BlockSpec((1, D, tn), lambda l, j: (l, 0, j)),
            pl.BlockSpec((1, 1, tn), lambda l, j: (l, 0, j)),
        ],
        out_specs=pl.BlockSpec((1, 8, tn), lambda l, j: (l, 0, j)),
        out_shape=jax.ShapeDtypeStruct((depth, 8, n), F32),
        compiler_params=_cp(("arbitrary", "arbitrary")),
        name="ada_mod",
    )(c_pad, ada_w, ada_b.reshape(depth, 1, n))


def _norm_rows(x, g, a, b):
    ms = jnp.mean(x * x, axis=-1, keepdims=True)
    return (x * lax.rsqrt(ms + EPS) * g) * a + b


def _norm_prologue(x_ref, g_ref, sc_ref, sh_ref, h_ref, rows=256):
    g = g_ref[...]
    a = 1.0 + sc_ref[0]
    b = sh_ref[0]
    for r0 in range(0, x_ref.shape[0], rows):
        h_ref[r0:r0 + rows, :] = _norm_rows(x_ref[r0:r0 + rows, :], g, a, b).astype(BF16)


def _norm_specs(tm):
    return [
        pl.BlockSpec((tm, D), lambda i, j: (i, 0)),
        pl.BlockSpec((1, D), lambda i, j: (0, 0)),
        pl.BlockSpec((1, 1, D), lambda i, j: (i * tm // SEQ, 0, 0)),
        pl.BlockSpec((1, 1, D), lambda i, j: (i * tm // SEQ, 0, 0)),
    ]


def _gmlp_in_kernel(x_ref, g_ref, sc_ref, sh_ref, w_ref, b_ref, z_ref, st_ref,
                    h_scr, s1_scr, s2_scr, *, nj, jv0, nv):
    j = pl.program_id(1)

    @pl.when(j == 0)
    def _():
        _norm_prologue(x_ref, g_ref, sc_ref, sh_ref, h_scr)
        s1_scr[...] = jnp.zeros_like(s1_scr)
        s2_scr[...] = jnp.zeros_like(s2_scr)

    acc = jnp.dot(h_scr[...], w_ref[...], preferred_element_type=F32) + b_ref[...]
    z = 0.5 * acc * (1.0 + lax.erf(acc * (1.0 / math.sqrt(2.0))))
    z_ref[...] = z.astype(BF16)

    @pl.when(j >= jv0)
    def _():
        s1_scr[...] += jnp.sum(z, axis=-1, keepdims=True)
        s2_scr[...] += jnp.sum(z * z, axis=-1, keepdims=True)

    @pl.when(j == nj - 1)
    def _():
        mean = s1_scr[...] * (1.0 / nv)
        var = jnp.maximum(s2_scr[...] * (1.0 / nv) - mean * mean, 0.0)
        rstd = lax.rsqrt(var + EPS)
        lane = lax.broadcasted_iota(jnp.int32, st_ref.shape, 1)
        st_ref[...] = jnp.where(lane == 0, mean, jnp.where(lane == 1, rstd, 0.0))


def _gmlp_in(x, g, sc, sh, w, b):
    n = w.shape[1]
    nj = n // TNG
    kern = functools.partial(_gmlp_in_kernel, nj=nj, jv0=GW // TNG, nv=float(GW))
    return pl.pallas_call(
        kern,
        grid=(T // TM, nj),
        in_specs=_norm_specs(TM) + [
            pl.BlockSpec((D, TNG), lambda i, j: (0, j)),
            pl.BlockSpec((1, TNG), lambda i, j: (0, j)),
        ],
        out_specs=[
            pl.BlockSpec((TM, TNG), lambda i, j: (i, j)),
            pl.BlockSpec((TM, LANES), lambda i, j: (i, 0)),
        ],
        out_shape=[
            jax.ShapeDtypeStruct((T, n), BF16),
            jax.ShapeDtypeStruct((T, LANES), F32),
        ],
        scratch_shapes=[
            pltpu.VMEM((TM, D), BF16),
            pltpu.VMEM((TM, 1), F32),
            pltpu.VMEM((TM, 1), F32),
        ],
        compiler_params=_cp(("arbitrary", "arbitrary")),
        name="gmlp_in",
    )(x, g, sc, sh, w, b.reshape(1, n))


def _gmlp_gate_kernel(u_ref, v_ref, st_ref, lng_ref, lnb_ref, ws_ref, bs_ref, o_ref):
    row = lax.broadcasted_iota(jnp.int32, (CHUNK, CHUNK), 0)
    col = lax.broadcasted_iota(jnp.int32, (CHUNK, CHUNK), 1)
    causal = row >= col
    ws = [jnp.where(causal, ws_ref[g], 0.0).astype(BF16) for g in range(GROUPS)]
    for c in range(TMG // CHUNK):
        rs = slice(c * CHUNK, (c + 1) * CHUNK)
        mean = st_ref[rs, 0:1]
        rstd = st_ref[rs, 1:2]
        for g in range(GROUPS):
            cs = slice(g * GDIM, (g + 1) * GDIM)
            vn = (v_ref[rs, cs].astype(F32) - mean) * rstd * lng_ref[:, cs] + lnb_ref[:, cs]
            mixed = jnp.dot(ws[g], vn.astype(BF16), preferred_element_type=F32) + bs_ref[g]
            o_ref[rs, cs] = (u_ref[rs, cs].astype(F32) * mixed).astype(BF16)


def _gmlp_gate(z, st, ln_g, ln_b, w_s, b_s):
    return pl.pallas_call(
        _gmlp_gate_kernel,
        grid=(T // TMG,),
        in_specs=[
            pl.BlockSpec((TMG, GW), lambda i: (i, 0)),
            pl.BlockSpec((TMG, GW), lambda i: (i, 1)),
            pl.BlockSpec((TMG, LANES), lambda i: (i, 0)),
            pl.BlockSpec((1, GW), lambda i: (0, 0)),
            pl.BlockSpec((1, GW), lambda i: (0, 0)),
            pl.BlockSpec((GROUPS, CHUNK, CHUNK), lambda i: (0, 0, 0)),
            pl.BlockSpec((GROUPS, CHUNK, 1), lambda i: (0, 0, 0)),
        ],
        out_specs=pl.BlockSpec((TMG, GW), lambda i: (i, 0)),
        out_shape=jax.ShapeDtypeStruct((T, GW), BF16),
        compiler_params=_cp(("arbitrary",)),
        name="gmlp_gate",
    )(z, z, st, ln_g.reshape(1, GW), ln_b.reshape(1, GW), w_s, b_s.reshape(GROUPS, CHUNK, 1))


def _res_gemm_kernel(a_ref, w_ref, x_ref, gate_ref, o_ref):
    y = jnp.dot(a_ref[...], w_ref[...], preferred_element_type=F32)
    o_ref[...] = x_ref[...] + gate_ref[0] * y


def _res_gemm(a, w, x, gate, *, tn, name):
    k = a.shape[1]
    return pl.pallas_call(
        _res_gemm_kernel,
        grid=(T // TM, D // tn),
        in_specs=[
            pl.BlockSpec((TM, k), lambda i, j: (i, 0)),
            pl.BlockSpec((k, tn), lambda i, j: (0, j)),
            pl.BlockSpec((TM, tn), lambda i, j: (i, j)),
            pl.BlockSpec((1, 1, tn), lambda i, j: (i * TM // SEQ, 0, j)),
        ],
        out_specs=pl.BlockSpec((TM, tn), lambda i, j: (i, j)),
        out_shape=jax.ShapeDtypeStruct((T, D), F32),
        compiler_params=_cp(("arbitrary", "arbitrary")),
        name=name,
    )(a, w, x, gate)


def _ffn_in_kernel(x_ref, g_ref, sc_ref, sh_ref, wg_ref, wu_ref, o_ref, h_scr):
    @pl.when(pl.program_id(1) == 0)
    def _():
        _norm_prologue(x_ref, g_ref, sc_ref, sh_ref, h_scr)

    h = h_scr[...]
    gate = jnp.dot(h, wg_ref[...], preferred_element_type=F32)
    up = jnp.dot(h, wu_ref[...], preferred_element_type=F32)
    o_ref[...] = (gate * jax.nn.sigmoid(gate) * up).astype(BF16)


def _ffn_in(x, g, sc, sh, w_gu):
    nj = DFF // TN
    return pl.pallas_call(
        _ffn_in_kernel,
        grid=(T // TM, nj),
        in_specs=_norm_specs(TM) + [
            pl.BlockSpec((D, TN), lambda i, j: (0, j)),
            pl.BlockSpec((D, TN), lambda i, j: (0, j + nj)),
        ],
        out_specs=pl.BlockSpec((TM, TN), lambda i, j: (i, j)),
        out_shape=jax.ShapeDtypeStruct((T, DFF), BF16),
        scratch_shapes=[pltpu.VMEM((TM, D), BF16)],
        compiler_params=_cp(("arbitrary", "arbitrary")),
        name="ffn_in",
    )(x, g, sc, sh, w_gu, w_gu)


def _qkv_kernel(x_ref, g_ref, sc_ref, sh_ref, pos_ref, invf_ref, w_ref, gain_ref, flag_ref,
                ind_ref, indt_ref, q_ref, k_ref, v_ref, h_scr, ct_scr, s1_scr, s2_scr):
    @pl.when(pl.program_id(1) == 0)
    def _():
        _norm_prologue(x_ref, g_ref, sc_ref, sh_ref, h_scr)
        lane = lax.broadcasted_iota(jnp.int32, ct_scr.shape, 1)
        m = lane & (HD - 1)
        ang = pos_ref[...] * invf_ref[...]
        c = jnp.cos(ang)
        s = jnp.sin(ang)
        ct_scr[...] = jnp.where(m < ROT, c, 1.0)
        s1_scr[...] = jnp.where(m < ROT // 2, -s, 0.0)
        s2_scr[...] = jnp.where((m >= ROT // 2) & (m < ROT), s, 0.0)

    y = jnp.dot(h_scr[...], w_ref[...], preferred_element_type=F32)
    ss = jnp.dot((y * y).astype(BF16), ind_ref[...], preferred_element_type=F32)
    r = lax.rsqrt(ss * (1.0 / HD) + EPS)
    rhi = r.astype(BF16)
    rlo = (r - rhi.astype(F32)).astype(BF16)
    indt = indt_ref[...]
    rb = (jnp.dot(rhi, indt, preferred_element_type=F32)
          + jnp.dot(rlo, indt, preferred_element_type=F32))
    yn = y * rb * gain_ref[...]
    ct = ct_scr[...]
    s1 = s1_scr[...]
    s2 = s2_scr[...]
    parts = []
    for k in range(TN // LANES):
        yb = yn[:, k * LANES:(k + 1) * LANES]
        parts.append(yb * ct + pltpu.roll(yb, LANES - ROT // 2, 1) * s1
                     + pltpu.roll(yb, ROT // 2, 1) * s2)
    yr = jnp.concatenate(parts, axis=1)
    res = jnp.where(flag_ref[...] > 0.5, yr, y).astype(BF16)

    j = pl.program_id(1)

    @pl.when(j < NKV)
    def _():
        for blk in range(TM // BLK):
            rs = slice(blk * BLK, (blk + 1) * BLK)
            for gq in range(GQA):
                q_ref[blk, 0, gq * BLK:(gq + 1) * BLK, :] = res[rs, gq * HD:(gq + 1) * HD]

    @pl.when(j == NKV)
    def _():
        for blk in range(TM // BLK):
            rs = slice(blk * BLK, (blk + 1) * BLK)
            for kv in range(NKV):
                k_ref[blk, kv] = res[rs, kv * HD:(kv + 1) * HD]
                v_ref[blk, kv] = res[rs, (NKV + kv) * HD:(NKV + kv + 1) * HD]


def _qkv(x, g, sc, sh, pos, invf, w, gain, flag, ind, indt):
    nblk = TM // BLK
    return pl.pallas_call(
        _qkv_kernel,
        grid=(T // TM, QKV // TN),
        in_specs=_norm_specs(TM) + [
            pl.BlockSpec((TM, 1), lambda i, j: (i, 0)),
            pl.BlockSpec((1, LANES), lambda i, j: (0, 0)),
            pl.BlockSpec((D, TN), lambda i, j: (0, j)),
            pl.BlockSpec((1, TN), lambda i, j: (0, j)),
            pl.BlockSpec((1, TN), lambda i, j: (0, j)),
            pl.BlockSpec((TN, LANES), lambda i, j: (0, 0)),
            pl.BlockSpec((LANES, TN), lambda i, j: (0, 0)),
        ],
        out_specs=[
            pl.BlockSpec((nblk, 1, GQA * BLK, HD), lambda i, j: (i, jnp.minimum(j, NKV - 1), 0, 0)),
            pl.BlockSpec((nblk, NKV, BLK, HD), lambda i, j: (i, 0, 0, 0)),
            pl.BlockSpec((nblk, NKV, BLK, HD), lambda i, j: (i, 0, 0, 0)),
        ],
        out_shape=[
            jax.ShapeDtypeStruct((BATCH * NB, NKV, GQA * BLK, HD), BF16),
            jax.ShapeDtypeStruct((BATCH * NB, NKV, BLK, HD), BF16),
            jax.ShapeDtypeStruct((BATCH * NB, NKV, BLK, HD), BF16),
        ],
        scratch_shapes=[
            pltpu.VMEM((TM, D), BF16),
            pltpu.VMEM((TM, LANES), F32),
            pltpu.VMEM((TM, LANES), F32),
            pltpu.VMEM((TM, LANES), F32),
        ],
        compiler_params=_cp(("arbitrary", "arbitrary")),
        name="qkv",
    )(x, g, sc, sh, pos, invf, w, gain, flag, ind, indt)


def _attn_kernel(q_ref, kp_ref, kc_ref, vp_ref, vc_ref, sink_ref, o_ref, bias_scr):
    n = pl.program_id(1)
    ri = lax.broadcasted_iota(jnp.int32, (2 * BLK, BLK), 0)
    qi = lax.broadcasted_iota(jnp.int32, (2 * BLK, BLK), 1)
    valid = (ri > qi) & (ri <= qi + BLK) & ((ri >= BLK) | (n > 0))
    bias_scr[...] = jnp.where(valid, 0.0, MASK_VALUE)
    for kv in range(NKV):
        q = q_ref[0, kv]
        kw = jnp.concatenate([kp_ref[0, kv], kc_ref[0, kv]], axis=0)
        vw = jnp.concatenate([vp_ref[0, kv], vc_ref[0, kv]], axis=0)
        st = lax.dot_general(kw, q, (((1,), (1,)), ((), ())), preferred_element_type=F32)
        st = st + jnp.concatenate([bias_scr[...]] * GQA, axis=1)
        sink = sink_ref[kv]
        m = jnp.maximum(jnp.max(st, axis=0, keepdims=True), sink)
        p = jnp.exp(st - m)
        den = jnp.sum(p, axis=0, keepdims=True) + jnp.exp(sink - m)
        ot = lax.dot_general(vw, p.astype(BF16), (((0,), (0,)), ((), ())),
                             preferred_element_type=F32)
        ot = ot * (1.0 / den)
        for gq in range(GQA):
            head = kv * GQA + gq
            o_ref[:, head * HD:(head + 1) * HD] = ot[:, gq * BLK:(gq + 1) * BLK].T.astype(BF16)


def _attention(q5, k5, v5, sink5):
    cur = lambda b, n: (b * NB + n, 0, 0, 0)
    prev = lambda b, n: (b * NB + jnp.maximum(n - 1, 0), 0, 0, 0)
    return pl.pallas_call(
        _attn_kernel,
        grid=(BATCH, NB),
        in_specs=[
            pl.BlockSpec((1, NKV, GQA * BLK, HD), cur),
            pl.BlockSpec((1, NKV, BLK, HD), prev),
            pl.BlockSpec((1, NKV, BLK, HD), cur),
            pl.BlockSpec((1, NKV, BLK, HD), prev),
            pl.BlockSpec((1, NKV, BLK, HD), cur),
            pl.BlockSpec((NKV, 1, GQA * BLK), lambda b, n: (0, 0, 0)),
        ],
        out_specs=pl.BlockSpec((BLK, NH * HD), lambda b, n: (b * NB + n, 0)),
        out_shape=jax.ShapeDtypeStruct((T, NH * HD), BF16),
        scratch_shapes=[pltpu.VMEM((2 * BLK, BLK), F32)],
        compiler_params=_cp(("arbitrary", "arbitrary")),
        name="attention",
    )(q5, k5, k5, v5, v5, sink5)


def _router_kernel(x_ref, g_ref, sc_ref, sh_ref, rw_ref, rb_ref, route_ref, cnt_ref, run_scr):
    i = pl.program_id(0)

    @pl.when(i == 0)
    def _():
        run_scr[...] = jnp.zeros_like(run_scr)

    h = _norm_rows(x_ref[...], g_ref[...], 1.0 + sc_ref[0], sh_ref[0])

    rw = rw_ref[...]
    h_hi = h.astype(BF16)
    h_lo = (h - h_hi.astype(F32)).astype(BF16)
    w_hi = rw.astype(BF16)
    w_lo = (rw - w_hi.astype(F32)).astype(BF16)
    logits = (jnp.dot(h_hi, w_hi, preferred_element_type=F32)
              + jnp.dot(h_lo, w_hi, preferred_element_type=F32)
              + jnp.dot(h_hi, w_lo, preferred_element_type=F32)) + rb_ref[...]
    lane = lax.broadcasted_iota(jnp.int32, logits.shape, 1).astype(F32)
    neg = -jnp.inf
    lg = jnp.where(lane < NE, logits, neg)
    v0 = jnp.max(lg, axis=-1, keepdims=True)
    i0 = jnp.min(jnp.where(lg == v0, lane, float(LANES)), axis=-1, keepdims=True)
    lg2 = jnp.where(lane == i0, neg, lg)
    v1 = jnp.max(lg2, axis=-1, keepdims=True)
    i1 = jnp.min(jnp.where(lg2 == v1, lane, float(LANES)), axis=-1, keepdims=True)
    t = jnp.exp(v1 - v0)
    w0 = 1.0 / (1.0 + t)
    w1 = t / (1.0 + t)

    sel0 = lane == i0
    sel1 = lane == i1
    onehot = jnp.where(sel0 | sel1, 1.0, 0.0)
    tm = onehot.shape[0]
    rr = lax.broadcasted_iota(jnp.int32, (tm, tm), 0)
    cc = lax.broadcasted_iota(jnp.int32, (tm, tm), 1)
    lower = jnp.where(rr > cc, 1.0, 0.0).astype(BF16)
    before = jnp.dot(lower, onehot.astype(BF16), preferred_element_type=F32) + run_scr[...]
    rank0 = jnp.sum(jnp.where(sel0, before, 0.0), axis=-1, keepdims=True)
    rank1 = jnp.sum(jnp.where(sel1, before, 0.0), axis=-1, keepdims=True)
    run_scr[...] += jnp.sum(onehot, axis=0, keepdims=True)
    cnt_ref[...] = run_scr[...]

    route_ref[...] = jnp.where(
        lane == 0, i0, jnp.where(
            lane == 1, i1, jnp.where(
                lane == 2, rank0, jnp.where(
                    lane == 3, rank1, jnp.where(
                        lane == 4, w0, jnp.where(lane == 5, w1, 0.0))))))


def _router(x, g, sc, sh, rw_pad, rb_pad):
    return pl.pallas_call(
        _router_kernel,
        grid=(T // TMR,),
        in_specs=[
            pl.BlockSpec((TMR, D), lambda i: (i, 0)),
            pl.BlockSpec((1, D), lambda i: (0, 0)),
            pl.BlockSpec((1, 1, D), lambda i: (i * TMR // SEQ, 0, 0)),
            pl.BlockSpec((1, 1, D), lambda i: (i * TMR // SEQ, 0, 0)),
            pl.BlockSpec((D, LANES), lambda i: (0, 0)),
            pl.BlockSpec((1, LANES), lambda i: (0, 0)),
        ],
        out_specs=[
            pl.BlockSpec((TMR, LANES), lambda i: (i, 0)),
            pl.BlockSpec((1, LANES), lambda i: (0, 0)),
        ],
        out_shape=[
            jax.ShapeDtypeStruct((T, LANES), F32),
            jax.ShapeDtypeStruct((1, LANES), F32),
        ],
        scratch_shapes=[pltpu.VMEM((1, LANES), F32)],
        compiler_params=_cp(("arbitrary",)),
        name="router",
    )(x, g, sc, sh, rw_pad, rb_pad)


def _dispatch_kernel(p0_ref, p1_ref, zs_ref, nu_ref, x_ref, g_ref, sc_ref, sh_ref, xs_hbm,
                     hbuf, sem):
    i = pl.program_id(0)
    n = pl.num_programs(0)
    slot = i % 2

    def drain(s):
        for _ in range(2):
            pltpu.make_async_copy(hbuf.at[s], xs_hbm.at[pl.ds(0, TMD)], sem.at[s]).wait()

    def zero_tile(row0):
        copies = [pltpu.make_async_copy(hbuf.at[half], xs_hbm.at[pl.ds(row0 + half * TMD, TMD)],
                                        sem.at[half]) for half in range(2)]
        for cp in copies:
            cp.start()
        for cp in copies:
            cp.wait()

    @pl.when(i == 0)
    def _():
        hbuf[...] = jnp.zeros_like(hbuf)
        for e in range(NE):
            zero_tile(pl.multiple_of(zs_ref[e], TME))

        def tail(t, carry):
            zero_tile(pl.multiple_of(t * TME, TME))
            return carry
        lax.fori_loop(nu_ref[0], NT, tail, 0)

    @pl.when(i >= 2)
    def _():
        drain(slot)

    hbuf[slot] = _norm_rows(x_ref[...], g_ref[...], 1.0 + sc_ref[0], sh_ref[0])

    def issue(t, carry):
        tok = i * TMD + t
        src = hbuf.at[slot, pl.ds(t, 1)]
        pltpu.make_async_copy(src, xs_hbm.at[pl.ds(p0_ref[tok], 1)],
                              sem.at[slot]).start(priority=0)
        pltpu.make_async_copy(src, xs_hbm.at[pl.ds(p1_ref[tok], 1)],
                              sem.at[slot]).start(priority=1)
        return carry
    lax.fori_loop(0, TMD, issue, 0, unroll=8)

    @pl.when(i == n - 1)
    def _():
        drain(1 - slot)
        drain(slot)


def _dispatch(pos0, pos1, zstart, nu, x, g, sc, sh):
    assert TME == 2 * TMD
    return pl.pallas_call(
        _dispatch_kernel,
        grid_spec=pltpu.PrefetchScalarGridSpec(
            num_scalar_prefetch=4,
            grid=(T // TMD,),
            in_specs=[
                pl.BlockSpec((TMD, D), lambda i, *_: (i, 0)),
                pl.BlockSpec((1, D), lambda i, *_: (0, 0)),
                pl.BlockSpec((1, 1, D), lambda i, *_: (i * TMD // SEQ, 0, 0)),
                pl.BlockSpec((1, 1, D), lambda i, *_: (i * TMD // SEQ, 0, 0)),
            ],
            out_specs=pl.BlockSpec(memory_space=pl.ANY),
            scratch_shapes=[pltpu.VMEM((2, TMD, D), F32), pltpu.SemaphoreType.DMA((2,))],
        ),
        out_shape=jax.ShapeDtypeStruct((R, D), F32),
        compiler_params=_cp(("arbitrary",)),
        name="moe_dispatch",
    )(pos0, pos1, zstart, nu, x, g, sc, sh)


def _absmax(ref, rows):
    m = None
    for r0 in range(0, ref.shape[0], rows):
        c = jnp.max(jnp.abs(ref[r0:r0 + rows, :]), axis=0, keepdims=True)
        m = c if m is None else jnp.maximum(m, c)
    return jnp.max(m.astype(F32), axis=1, keepdims=True)


def _f8_scale(amax):
    return jnp.where(amax > 0.0, (0.5 * F8_MAX) / amax, 1.0)


def _to_f8(src_ref, dst_ref, scale, rows):
    for r0 in range(0, src_ref.shape[0], rows):
        dst_ref[r0:r0 + rows, :] = (src_ref[r0:r0 + rows, :].astype(F32) * scale).astype(F8)


def _splat(s):
    return jnp.broadcast_to(s, (8, LANES))


def _moe_up_kernel(te_ref, first_ref, nu_ref, xs_ref, wg_ref, wu_ref, o_ref, so_ref,
                   wg_scr, wu_scr, x_scr, sw_scr, sx_scr):
    j = pl.program_id(0)
    i = pl.program_id(1)

    @pl.when(first_ref[i] == 1)
    def _():
        for k, (w_ref, w_scr) in enumerate(((wg_ref, wg_scr), (wu_ref, wu_scr))):
            s = _f8_scale(_absmax(w_ref.at[0], 256))
            _to_f8(w_ref.at[0], w_scr, s, 256)
            sw_scr[k] = _splat(s)

    @pl.when((i < nu_ref[0]) & (j == 0))
    def _():
        sx_scr[i] = _splat(_f8_scale(_absmax(xs_ref, 128)))

    @pl.when(i < nu_ref[0])
    def _():
        sx = sx_scr[i][0:1, 0:1]
        _to_f8(xs_ref, x_scr, sx, 128)
        x = x_scr[...]
        gate = jnp.dot(x, wg_scr[...], preferred_element_type=F32) * (1.0 / (sx * sw_scr[0][0:1, 0:1]))
        up = jnp.dot(x, wu_scr[...], preferred_element_type=F32) * (1.0 / (sx * sw_scr[1][0:1, 0:1]))
        act = gate * jax.nn.sigmoid(gate) * up
        amax = jnp.max(jnp.max(jnp.abs(act), axis=0, keepdims=True), axis=1, keepdims=True)
        so = _f8_scale(amax)
        o_ref[...] = (act * so).astype(F8)
        so_ref[0, 0] = _splat(so)

    @pl.when(i >= nu_ref[0])
    def _():
        o_ref[...] = jnp.zeros_like(o_ref)
        so_ref[0, 0] = jnp.ones((8, LANES), F32)


def _moe_up(te, first, nu, xs, w_gu):
    nj = DFE // TNE1
    tile = lambda j, i, te, first, nu: jnp.maximum(jnp.minimum(i, nu[0] - 1), 0)
    return pl.pallas_call(
        _moe_up_kernel,
        grid_spec=pltpu.PrefetchScalarGridSpec(
            num_scalar_prefetch=3,
            grid=(nj, NT),
            in_specs=[
                pl.BlockSpec((TME, D), lambda *a: (tile(*a), 0)),
                pl.BlockSpec((1, D, TNE1), lambda j, i, te, first, nu: (te[i], 0, j)),
                pl.BlockSpec((1, D, TNE1), lambda j, i, te, first, nu: (te[i], 0, j + nj)),
            ],
            out_specs=[
                pl.BlockSpec((TME, TNE1), lambda j, i, te, first, nu: (i, j)),
                pl.BlockSpec((1, 1, 8, LANES), lambda j, i, te, first, nu: (i, j, 0, 0)),
            ],
            scratch_shapes=[
                pltpu.VMEM((D, TNE1), F8),
                pltpu.VMEM((D, TNE1), F8),
                pltpu.VMEM((TME, D), F8),
                pltpu.VMEM((2, 8, LANES), F32),
                pltpu.VMEM((NT, 8, LANES), F32),
            ],
        ),
        out_shape=[
            jax.ShapeDtypeStruct((R, DFE), F8),
            jax.ShapeDtypeStruct((NT, nj, 8, LANES), F32),
        ],
        compiler_params=_cp(("arbitrary", "arbitrary")),
        name="moe_up",
    )(te, first, nu, xs, w_gu, w_gu)


def _moe_down_kernel(te_ref, first_ref, nu_ref, a_ref, sa_ref, w_ref, o_ref, w_scr, sw_scr):
    i = pl.program_id(1)

    @pl.when(first_ref[i] == 1)
    def _():
        s = _f8_scale(_absmax(w_ref.at[0], 512))
        _to_f8(w_ref.at[0], w_scr, s, 512)
        sw_scr[...] = _splat(s)

    @pl.when(i < nu_ref[0])
    def _():
        sw = sw_scr[0:1, 0:1]
        acc = None
        for c in range(DFE // TNE1):
            ks = slice(c * TNE1, (c + 1) * TNE1)
            part = jnp.dot(a_ref[:, ks], w_scr[ks, :], preferred_element_type=F32)
            part = part * (1.0 / (sa_ref[0, c][0:1, 0:1] * sw))
            acc = part if acc is None else acc + part
        o_ref[...] = acc

    @pl.when(i >= nu_ref[0])
    def _():
        o_ref[...] = jnp.zeros_like(o_ref)


def _moe_down(te, first, nu, a, sa, w_d):
    tile = lambda j, i, te, first, nu: jnp.maximum(jnp.minimum(i, nu[0] - 1), 0)
    return pl.pallas_call(
        _moe_down_kernel,
        grid_spec=pltpu.PrefetchScalarGridSpec(
            num_scalar_prefetch=3,
            grid=(D // TNE2, NT),
            in_specs=[
                pl.BlockSpec((TME, DFE), lambda *a: (tile(*a), 0)),
                pl.BlockSpec((1, DFE // TNE1, 8, LANES), lambda *a: (tile(*a), 0, 0, 0)),
                pl.BlockSpec((1, DFE, TNE2), lambda j, i, te, first, nu: (te[i], 0, j)),
            ],
            out_specs=pl.BlockSpec((TME, TNE2), lambda j, i, te, first, nu: (i, j)),
            scratch_shapes=[
                pltpu.VMEM((DFE, TNE2), F8),
                pltpu.VMEM((8, LANES), F32),
            ],
        ),
        out_shape=jax.ShapeDtypeStruct((R, D), F32),
        compiler_params=_cp(("arbitrary", "arbitrary")),
        name="moe_down",
    )(te, first, nu, a, sa, w_d)


def _combine_kernel(p0_ref, p1_ref, ys_hbm, x_ref, route_ref, gate_ref, o_ref, buf, sem):
    i = pl.program_id(0)
    n = pl.num_programs(0)

    def row_copy(src, k, t, slot):
        return pltpu.make_async_copy(ys_hbm.at[pl.ds(src, 1)], buf.at[slot, k, pl.ds(t, 1)],
                                     sem.at[slot])

    def issue(step, slot):
        def body(t, carry):
            tok = step * TMC + t
            row_copy(p0_ref[tok], 0, t, slot).start(priority=0)
            row_copy(p1_ref[tok], 1, t, slot).start(priority=1)
            return carry
        lax.fori_loop(0, TMC, body, 0, unroll=8)

    def drain(slot):
        for k in range(2):
            pltpu.make_async_copy(ys_hbm.at[pl.ds(0, TMC)], buf.at[slot, k], sem.at[slot]).wait()

    @pl.when(i == 0)
    def _():
        issue(0, 0)

    @pl.when(i + 1 < n)
    def _():
        issue(i + 1, (i + 1) % 2)

    slot = i % 2
    drain(slot)
    w0 = route_ref[:, 4:5]
    w1 = route_ref[:, 5:6]
    y = w0 * buf[slot, 0] + w1 * buf[slot, 1]
    o_ref[...] = x_ref[...] + gate_ref[0] * y


def _combine(pos0, pos1, ys, x, route, gate):
    return pl.pallas_call(
        _combine_kernel,
        grid_spec=pltpu.PrefetchScalarGridSpec(
            num_scalar_prefetch=2,
            grid=(T // TMC,),
            in_specs=[
                pl.BlockSpec(memory_space=pl.ANY),
                pl.BlockSpec((TMC, D), lambda i, p0, p1: (i, 0)),
                pl.BlockSpec((TMC, LANES), lambda i, p0, p1: (i, 0)),
                pl.BlockSpec((1, 1, D), lambda i, p0, p1: (i * TMC // SEQ, 0, 0)),
            ],
            out_specs=pl.BlockSpec((TMC, D), lambda i, p0, p1: (i, 0)),
            scratch_shapes=[
                pltpu.VMEM((2, 2, TMC, D), F32),
                pltpu.SemaphoreType.DMA((2,)),
            ],
        ),
        out_shape=jax.ShapeDtypeStruct((T, D), F32),
        compiler_params=_cp(("arbitrary",)),
        name="moe_combine",
    )(pos0, pos1, ys, x, route, gate)


def _routing_plan(route, counts):
    e0 = route[:, 0].astype(jnp.int32)
    e1 = route[:, 1].astype(jnp.int32)
    r0 = route[:, 2].astype(jnp.int32)
    r1 = route[:, 3].astype(jnp.int32)
    cnt = counts[0, :NE].astype(jnp.int32)
    ntile = (cnt + TME - 1) // TME
    tile_end = jnp.cumsum(ntile)
    offs = (tile_end - ntile) * TME
    nu = tile_end[-1]
    tid = jnp.arange(NT, dtype=jnp.int32)
    te_raw = jnp.sum((tid[:, None] >= tile_end[None, :]).astype(jnp.int32), axis=1)
    te_last = jnp.sum((nu - 1 >= tile_end).astype(jnp.int32))
    te = jnp.where(tid < nu, te_raw, te_last).astype(jnp.int32)
    prev = jnp.concatenate([jnp.full((1,), -1, jnp.int32), te[:-1]])
    first = ((te != prev) & (tid < nu)).astype(jnp.int32)
    pos0 = offs[e0] + r0
    pos1 = offs[e1] + r1
    zstart = jnp.maximum(tile_end - 1, 0).astype(jnp.int32) * TME
    return pos0, pos1, te, first, nu.reshape(1).astype(jnp.int32), zstart


def kernel(x, c, positions, ada_w, ada_b, norm_g, gmlp_w_in, gmlp_b_in, gmlp_ln_g, gmlp_ln_b,
           gmlp_w_s, gmlp_b_s, gmlp_w_out, attn_w_qkv, attn_q_norm_g, attn_k_norm_g, attn_sinks,
           attn_w_o, ffn_w_gate_up, ffn_w_down, moe_router_w, moe_router_b, moe_w_gate_up,
           moe_w_down):
    xf = x.reshape(T, D)
    c_pad = jnp.concatenate([c, jnp.zeros((8 - BATCH, D), F32)], axis=0)
    mod = _ada_mod(c_pad, ada_w, ada_b)[:, :BATCH, :]

    def mods(layer):
        return [mod[layer, :, k * D:(k + 1) * D].reshape(BATCH, 1, D) for k in range(6)]

    sh1, sc1, g1, sh2, sc2, g2 = mods(0)
    z, st = _gmlp_in(xf, norm_g[0, 0].reshape(1, D), sc1, sh1, gmlp_w_in[0].astype(BF16),
                     gmlp_b_in[0])
    gated = _gmlp_gate(z, st, gmlp_ln_g[0], gmlp_ln_b[0], gmlp_w_s[0], gmlp_b_s[0])
    xf = _res_gemm(gated, gmlp_w_out[0].astype(BF16), xf, g1, tn=2 * TN, name="gmlp_out")
    act = _ffn_in(xf, norm_g[0, 1].reshape(1, D), sc2, sh2, ffn_w_gate_up[0].astype(BF16))
    xf = _res_gemm(act, ffn_w_down[0].astype(BF16), xf, g2, tn=TN, name="ffn_out")

    sh1, sc1, g1, sh2, sc2, g2 = mods(1)
    pos = positions.reshape(T, 1).astype(F32)
    inv_freq = THETA ** (-jnp.arange(0, ROT, 2, dtype=F32) / ROT)
    invf = jnp.tile(inv_freq, LANES // (ROT // 2)).reshape(1, LANES)
    scale = HD ** -0.5
    gain = jnp.concatenate([jnp.tile(attn_q_norm_g[0] * scale, NH),
                            jnp.tile(attn_k_norm_g[0], NKV),
                            jnp.ones((NKV * HD,), F32)]).reshape(1, QKV)
    flag = jnp.concatenate([jnp.ones(((NH + NKV) * HD,), F32),
                            jnp.zeros((NKV * HD,), F32)]).reshape(1, QKV)
    head_of_col = jnp.arange(TN, dtype=jnp.int32) // HD
    ind = (head_of_col[:, None] == jnp.arange(LANES, dtype=jnp.int32)[None, :]).astype(BF16)
    q5, k5, v5 = _qkv(xf, norm_g[1, 0].reshape(1, D), sc1, sh1, pos, invf,
                      attn_w_qkv[0].astype(BF16), gain, flag, ind, ind.T)
    sink5 = jnp.repeat(attn_sinks[0].astype(F32), BLK).reshape(NKV, 1, GQA * BLK)
    o = _attention(q5, k5, v5, sink5)
    xf = _res_gemm(o, attn_w_o[0].astype(BF16), xf, g1, tn=2 * TN, name="attn_out")

    rw_pad = jnp.concatenate([moe_router_w[0].astype(F32),
                              jnp.zeros((D, LANES - NE), F32)], axis=1)
    rb_pad = jnp.concatenate([moe_router_b[0].astype(F32),
                              jnp.zeros((LANES - NE,), F32)]).reshape(1, LANES)
    route, counts = _router(xf, norm_g[1, 1].reshape(1, D), sc2, sh2, rw_pad, rb_pad)
    pos0, pos1, te, first, nu, zstart = _routing_plan(route, counts)
    xs = _dispatch(pos0, pos1, zstart, nu, xf, norm_g[1, 1].reshape(1, D), sc2, sh2)
    hmid, hscale = _moe_up(te, first, nu, xs, moe_w_gate_up[0])
    ys = _moe_down(te, first, nu, hmid, hscale, moe_w_down[0])
    xf = _combine(pos0, pos1, ys, xf, route, g2)
    return xf.reshape(BATCH, SEQ, D)
```
